```python
import math
import jax, jax.numpy as jnp
from jax import lax
import numpy as np

D_MODEL = 2048
BATCH = 2
SEQ = 8192
DEPTH = 1

SSM_GROUP = 16
SSM_WIDTH = 1024
SSM_GROUPS = SSM_WIDTH // SSM_GROUP
SSM_STATE = 64
DT_MIN = 1e-3
DT_MAX = 1e-1
SB_HEADS = 8
SB_HEAD_DIM = 128
SB_WIDTH = SB_HEADS * SB_HEAD_DIM
Q_BLOCK = 128
IN_COLS = SSM_WIDTH + 3 * SB_WIDTH + 2 * D_MODEL
SPLITS = (SSM_WIDTH, SSM_WIDTH + SB_WIDTH, SSM_WIDTH + 2 * SB_WIDTH,
          SSM_WIDTH + 3 * SB_WIDTH, SSM_WIDTH + 3 * SB_WIDTH + D_MODEL)
PEER_HEADS = 8
PEER_N_KEYS = 128
PEER_N_EXPERTS = PEER_N_KEYS * PEER_N_KEYS
PEER_TOPK = 16
PEER_QUERY_DIM = 256
PEER_HALF = PEER_QUERY_DIM // 2
PEER_CHUNK = 128
N_ADA = 6
EPS = 1e-6

kernel_name = "hybrid_s5_stickbreak_peer_adaln"


def _rms(x, g):
    xf = x.astype(jnp.float32)
    y = xf * lax.rsqrt(jnp.mean(xf * xf, axis=-1, keepdims=True) + EPS)
    return (y * g.astype(jnp.float32)).astype(x.dtype)


def _ada_norm(x, g, shift, scale):
    return _rms(x, g) * (1 + scale) + shift


def _s5(u, lam_re, lam_im, log_dt, b_re, b_im, c_re, c_im, d_skip):
    f32 = jnp.float32
    uf = u.astype(f32)
    lam = lax.complex(lam_re.astype(f32), lam_im.astype(f32))
    dt = jnp.exp(log_dt.astype(f32))[:, None]
    lam_bar = jnp.exp(lam * dt)
    b = lax.complex(b_re.astype(f32), b_im.astype(f32))
    b_bar = ((lam_bar - 1) / lam)[..., None] * b
    cmat = lax.complex(c_re.astype(f32), c_im.astype(f32))
    bu = jnp.einsum('bsgi,gpi->bsgp', uf.astype(jnp.complex64), b_bar)
    a = jnp.broadcast_to(lam_bar, (1, uf.shape[1]) + lam_bar.shape)

    def combine(left, right):
        a_l, b_l = left
        a_r, b_r = right
        return a_r * a_l, a_r * b_l + b_r

    _, states = lax.associative_scan(combine, (a, bu), axis=1)
    y = jnp.real(jnp.einsum('bsgp,gip->bsgi', states, cmat)) + d_skip.astype(f32) * uf
    return y.astype(u.dtype)


def _stick_breaking(q, k, v):
    f32 = jnp.float32
    seq = q.shape[2]
    scale = SB_HEAD_DIM ** -0.5
    outs = []
    for blk in range(seq // Q_BLOCK):
        t0 = blk * Q_BLOCK
        length = t0 + Q_BLOCK
        qb = q[:, :, t0:length].astype(f32)
        kb = k[:, :, :length].astype(f32)
        vb = v[:, :, :length].astype(f32)
        z = jnp.einsum('bhqd,bhkd->bhqk', qb, kb) * scale
        t_idx = t0 + jnp.arange(Q_BLOCK)[:, None]
        s_idx = jnp.arange(length)[None, :]
        causal = s_idx < t_idx
        log_fail = jnp.where(causal, jax.nn.log_sigmoid(-z), 0.0)
        suffix = lax.cumsum(log_fail, axis=3, reverse=True) - log_fail
        log_w = jnp.where(causal, jax.nn.log_sigmoid(z) + suffix, -jnp.inf)
        outs.append(jnp.einsum('bhqk,bhkd->bhqd', jnp.exp(log_w), vb))
    return jnp.concatenate(outs, axis=2).astype(v.dtype)


def _peer(h, wq, k1, k2, u_tab, v_tab):
    f32 = jnp.float32
    bsz, seq, d = h.shape
    t = bsz * seq
    hf = h.reshape(t, d)
    q = (hf @ wq).reshape(t, PEER_HEADS, 2, PEER_HALF).astype(f32)
    s1 = jnp.einsum('thd,nd->thn', q[:, :, 0], k1.astype(f32))
    s2 = jnp.einsum('thd,nd->thn', q[:, :, 1], k2.astype(f32))
    v1, i1 = lax.top_k(s1, PEER_TOPK)
    v2, i2 = lax.top_k(s2, PEER_TOPK)
    cand = (v1[..., :, None] + v2[..., None, :]).reshape(t, PEER_HEADS, PEER_TOPK * PEER_TOPK)
    cidx = (i1[..., :, None] * PEER_N_KEYS + i2[..., None, :]).reshape(t, PEER_HEADS, PEER_TOPK * PEER_TOPK)
    best, pos = lax.top_k(cand, PEER_TOPK)
    idx = jnp.take_along_axis(cidx, pos, axis=-1)
    gate = jax.nn.softmax(best, axis=-1).astype(h.dtype)
    n_chunks = t // PEER_CHUNK

    def chunk(args):
        xc, ic, gc = args
        ue = jnp.take(u_tab, ic, axis=0)
        act = gc * jax.nn.gelu(jnp.einsum('cd,chkd->chk', xc, ue))
        ve = jnp.take(v_tab, ic, axis=0)
        return jnp.einsum('chk,chkd->cd', act, ve)

    out = lax.map(chunk, (hf.reshape(n_chunks, PEER_CHUNK, d),
                          idx.reshape(n_chunks, PEER_CHUNK, PEER_HEADS, PEER_TOPK),
                          gate.reshape(n_chunks, PEER_CHUNK, PEER_HEADS, PEER_TOPK)))
    return out.reshape(bsz, seq, d)


def setup_inputs(seed: int = 0) -> dict:
    key = jax.random.key(seed)
    ks = jax.random.split(key, 26)
    nrm = jax.random.normal
    D = D_MODEL
    L = DEPTH
    x = nrm(ks[0], (BATCH, SEQ, D), jnp.float32)
    c = nrm(ks[1], (BATCH, D), jnp.float32)
    w_ada = nrm(ks[2], (L, D, N_ADA * D), jnp.float32) * (0.5 * D ** -0.5)
    b_ada = nrm(ks[3], (L, N_ADA * D), jnp.float32) * 0.01
    norm1_g = 1.0 + 0.02 * nrm(ks[4], (L, D), jnp.float32)
    w_in = nrm(ks[5], (L, D, IN_COLS), jnp.float32) * D ** -0.5
    lam_re = -0.5 * jnp.exp(0.01 * nrm(ks[6], (L, SSM_GROUPS, SSM_STATE), jnp.float32))
    lam_im = (jnp.pi * jnp.arange(SSM_STATE, dtype=jnp.float32))[None, None, :] \
        + 0.01 * nrm(ks[7], (L, SSM_GROUPS, SSM_STATE), jnp.float32)
    log_dt = jax.random.uniform(ks[8], (L, SSM_GROUPS), jnp.float32,
                                math.log(DT_MIN), math.log(DT_MAX))
    ssm_b_re = nrm(ks[9], (L, SSM_GROUPS, SSM_STATE, SSM_GROUP), jnp.float32) * (2 * SSM_GROUP) ** -0.5
    ssm_b_im = nrm(ks[10], (L, SSM_GROUPS, SSM_STATE, SSM_GROUP), jnp.float32) * (2 * SSM_GROUP) ** -0.5
    ssm_c_re = nrm(ks[11], (L, SSM_GROUPS, SSM_GROUP, SSM_STATE), jnp.float32) * 0.5
    ssm_c_im = nrm(ks[12], (L, SSM_GROUPS, SSM_GROUP, SSM_STATE), jnp.float32) * 0.5
    ssm_d = nrm(ks[13], (L, SSM_GROUPS, SSM_GROUP), jnp.float32) * 0.5
    w_glu = nrm(ks[14], (L, SSM_WIDTH, 2 * D), jnp.float32) * SSM_WIDTH ** -0.5
    q_norm_g = 1.0 + 0.02 * nrm(ks[15], (L, SB_HEAD_DIM), jnp.float32)
    k_norm_g = 1.0 + 0.02 * nrm(ks[16], (L, SB_HEAD_DIM), jnp.float32)
    w_att_up = nrm(ks[17], (L, SB_WIDTH, D), jnp.float32) * SB_WIDTH ** -0.5
    w_out = nrm(ks[18], (L, D, D), jnp.float32) * D ** -0.5
    norm2_g = 1.0 + 0.02 * nrm(ks[19], (L, D), jnp.float32)
    peer_wq = nrm(ks[20], (L, D, PEER_HEADS * PEER_QUERY_DIM), jnp.float32) * D ** -0.5
    peer_k1 = nrm(ks[21], (L, PEER_N_KEYS, PEER_HALF), jnp.float32) * PEER_HALF ** -0.5
    peer_k2 = nrm(ks[22], (L, PEER_N_KEYS, PEER_HALF), jnp.float32) * PEER_HALF ** -0.5
    peer_u = nrm(ks[23], (L, PEER_N_EXPERTS, D), jnp.float32) * D ** -0.5
    peer_v = nrm(ks[24], (L, PEER_N_EXPERTS, D), jnp.float32)
    return {"x": x, "c": c, "w_ada": w_ada, "b_ada": b_ada, "norm1_g": norm1_g, "w_in": w_in,
            "lam_re": lam_re, "lam_im": lam_im, "log_dt": log_dt,
            "ssm_b_re": ssm_b_re, "ssm_b_im": ssm_b_im, "ssm_c_re": ssm_c_re, "ssm_c_im": ssm_c_im,
            "ssm_d": ssm_d, "w_glu": w_glu, "q_norm_g": q_norm_g, "k_norm_g": k_norm_g,
            "w_att_up": w_att_up, "w_out": w_out, "norm2_g": norm2_g, "peer_wq": peer_wq,
            "peer_k1": peer_k1, "peer_k2": peer_k2, "peer_u": peer_u, "peer_v": peer_v}


def reference(x, c, w_ada, b_ada, norm1_g, w_in, lam_re, lam_im, log_dt,
              ssm_b_re, ssm_b_im, ssm_c_re, ssm_c_im, ssm_d, w_glu, q_norm_g, k_norm_g,
              w_att_up, w_out, norm2_g, peer_wq, peer_k1, peer_k2, peer_u, peer_v):
    bsz, seq, _ = x.shape
    c_act = jax.nn.silu(c)
    for l in range(DEPTH):
        mod = c_act @ w_ada[l] + b_ada[l]
        sh1, sc1, gt1, sh2, sc2, gt2 = [m[:, None, :] for m in jnp.split(mod, N_ADA, axis=-1)]

        h = _ada_norm(x, norm1_g[l], sh1, sc1)
        proj = h @ w_in[l]
        u_ssm, q, k, v, g_ssm, g_att = jnp.split(proj, SPLITS, axis=-1)

        y = _s5(u_ssm.reshape(bsz, seq, SSM_GROUPS, SSM_GROUP), lam_re[l], lam_im[l], log_dt[l],
                ssm_b_re[l], ssm_b_im[l], ssm_c_re[l], ssm_c_im[l], ssm_d[l])
        y = jax.nn.gelu(y.reshape(bsz, seq, SSM_WIDTH))
        y_val, y_gate = jnp.split(y @ w_glu[l], 2, axis=-1)
        ssm_out = y_val * jax.nn.sigmoid(y_gate)

        qh = _rms(q.reshape(bsz, seq, SB_HEADS, SB_HEAD_DIM), q_norm_g[l]).transpose(0, 2, 1, 3)
        kh = _rms(k.reshape(bsz, seq, SB_HEADS, SB_HEAD_DIM), k_norm_g[l]).transpose(0, 2, 1, 3)
        vh = v.reshape(bsz, seq, SB_HEADS, SB_HEAD_DIM).transpose(0, 2, 1, 3)
        att = _stick_breaking(qh, kh, vh).transpose(0, 2, 1, 3).reshape(bsz, seq, SB_WIDTH)
        att_out = att @ w_att_up[l]

        merged = jax.nn.sigmoid(g_ssm) * ssm_out + jax.nn.sigmoid(g_att) * att_out
        x = x + gt1 * (merged @ w_out[l])

        h2 = _ada_norm(x, norm2_g[l], sh2, sc2)
        x = x + gt2 * _peer(h2, peer_wq[l], peer_k1[l], peer_k2[l], peer_u[l], peer_v[l])
    return x
```

```python
import functools
import math

import jax
import jax.numpy as jnp
from jax import lax
from jax.experimental import pallas as pl
from jax.experimental.pallas import tpu as pltpu

F32 = jnp.float32
BF16 = jnp.bfloat16

LANES = 128
SUBLANES = 8
VMEM_LIMIT = 56 * 1024 * 1024

SSM_GROUP = 16
SSM_STATE = 64
SB_HEADS = 8
SB_HEAD_DIM = 128
PEER_HEADS = 8
PEER_N_KEYS = 128
PEER_TOPK = 16
PEER_HALF = 128
N_ADA = 6
EPS = 1e-6
NEG_INF = float("-inf")


def _params(sem):
    return pltpu.CompilerParams(dimension_semantics=sem, vmem_limit_bytes=VMEM_LIMIT)


def _pick(n, pref):
    t = min(n, pref)
    while n % t:
        t //= 2
    return t


def _ada_kernel(c_ref, w_ref, b_ref, o_ref):
    c = c_ref[...]
    c = c * jax.nn.sigmoid(c)
    o_ref[...] = jnp.dot(c, w_ref[...], preferred_element_type=F32) + b_ref[...]


def _ada(c_pad, w, b):
    rows, d = c_pad.shape
    n = w.shape[1]
    tn = _pick(n, 1024)
    return pl.pallas_call(
        _ada_kernel,
        grid=(n // tn,),
        in_specs=[
            pl.BlockSpec((rows, d), lambda j: (0, 0)),
            pl.BlockSpec((d, tn), lambda j: (0, j)),
            pl.BlockSpec((1, tn), lambda j: (0, j)),
        ],
        out_specs=pl.BlockSpec((rows, tn), lambda j: (0, j)),
        out_shape=jax.ShapeDtypeStruct((rows, n), F32),
        compiler_params=_params(("arbitrary",)),
        name="ada",
    )(c_pad, w, b)


def _head_rms(acc, g_ref, o_ref, scale):
    for h in range(acc.shape[1] // SB_HEAD_DIM):
        blk = acc[:, h * SB_HEAD_DIM:(h + 1) * SB_HEAD_DIM]
        ms = jnp.mean(blk * blk, axis=-1, keepdims=True)
        y = blk * lax.rsqrt(ms + EPS) * g_ref[...]
        if scale != 1.0:
            y = y * scale
        o_ref[:, h * SB_HEAD_DIM:(h + 1) * SB_HEAD_DIM] = y.astype(o_ref.dtype)


def _inproj_kernel(x_ref, mod_ref, g_ref, w_ref, qg_ref, kg_ref, o_ref, h_scr, *, q_blk, k_blk):
    j = pl.program_id(1)

    @pl.when(j == 0)
    def _():
        x = x_ref[...]
        ms = jnp.mean(x * x, axis=-1, keepdims=True)
        y = x * lax.rsqrt(ms + EPS) * g_ref[...]
        h = y * (1.0 + mod_ref[0, 1:2, :]) + mod_ref[0, 0:1, :]
        h_scr[...] = h.astype(BF16)

    acc = jnp.dot(h_scr[...], w_ref[...], preferred_element_type=F32)

    @pl.when(j == q_blk)
    def _():
        _head_rms(acc, qg_ref, o_ref, SB_HEAD_DIM ** -0.5)

    @pl.when(j == k_blk)
    def _():
        _head_rms(acc, kg_ref, o_ref, 1.0)

    @pl.when(jnp.logical_and(j != q_blk, j != k_blk))
    def _():
        o_ref[...] = acc.astype(o_ref.dtype)


def _inproj(x2, mod3, g, w, qg, kg, seq, sb_width):
    t, d = x2.shape
    n = w.shape[1]
    tm = _pick(seq, 512)
    tn = sb_width
    steps_per_batch = seq // tm
    kern = functools.partial(_inproj_kernel, q_blk=1, k_blk=2)
    return pl.pallas_call(
        kern,
        grid=(t // tm, n // tn),
        in_specs=[
            pl.BlockSpec((tm, d), lambda i, j: (i, 0)),
            pl.BlockSpec((1, N_ADA, d), lambda i, j: (i // steps_per_batch, 0, 0)),
            pl.BlockSpec((1, d), lambda i, j: (0, 0)),
            pl.BlockSpec((d, tn), lambda i, j: (0, j)),
            pl.BlockSpec((1, SB_HEAD_DIM), lambda i, j: (0, 0)),
            pl.BlockSpec((1, SB_HEAD_DIM), lambda i, j: (0, 0)),
        ],
        out_specs=pl.BlockSpec((tm, tn), lambda i, j: (i, j)),
        out_shape=jax.ShapeDtypeStruct((t, n), BF16),
        scratch_shapes=[pltpu.VMEM((tm, d), BF16)],
        compiler_params=_params(("parallel", "arbitrary")),
        name="inproj",
    )(x2, mod3, g, w, qg, kg)


S5_CHUNKS = 8
S5_CHUNK_IN = LANES
S5_CHUNK_STATE = 8 * SSM_STATE
S5_SLABS = S5_CHUNK_STATE // LANES
S5_PITCH_PAD = SUBLANES


def _s5_kernel(u_ref, br_ref, bi_ref, cr_ref, ci_ref, lamr_ref, lami_ref, d_ref, o_ref,
               bur, bui, xrs, xis, sr, si, *, tm, pitch):
    @pl.when(pl.program_id(1) == 0)
    def _():
        sr[...] = jnp.zeros_like(sr)
        si[...] = jnp.zeros_like(si)

    for c in range(S5_CHUNKS):
        uc = u_ref[:, c * S5_CHUNK_IN:(c + 1) * S5_CHUNK_IN]
        r = jnp.dot(uc, br_ref[c], preferred_element_type=F32)
        im = jnp.dot(uc, bi_ref[c], preferred_element_type=F32)
        for l in range(S5_SLABS):
            bur[l, c * pitch:c * pitch + tm, :] = r[:, l * LANES:(l + 1) * LANES]
            bui[l, c * pitch:c * pitch + tm, :] = im[:, l * LANES:(l + 1) * LANES]

    lamr = [lamr_ref[:, l * LANES:(l + 1) * LANES] for l in range(S5_SLABS)]
    lami = [lami_ref[:, l * LANES:(l + 1) * LANES] for l in range(S5_SLABS)]

    def body(t, carry):
        xr, xi = carry
        nxr, nxi = [], []
        for l in range(S5_SLABS):
            rows = pl.ds(t, S5_CHUNKS, stride=pitch)
            b_r = bur[l, rows, :]
            b_i = bui[l, rows, :]
            n_r = lamr[l] * xr[l] - lami[l] * xi[l] + b_r
            n_i = lamr[l] * xi[l] + lami[l] * xr[l] + b_i
            xrs[l, rows, :] = n_r
            xis[l, rows, :] = n_i
            nxr.append(n_r)
            nxi.append(n_i)
        return tuple(nxr), tuple(nxi)

    init = (tuple(sr[:, l * LANES:(l + 1) * LANES] for l in range(S5_SLABS)),
            tuple(si[:, l * LANES:(l + 1) * LANES] for l in range(S5_SLABS)))
    xr, xi = lax.fori_loop(0, tm, body, init, unroll=8)
    for l in range(S5_SLABS):
        sr[:, l * LANES:(l + 1) * LANES] = xr[l]
        si[:, l * LANES:(l + 1) * LANES] = xi[l]

    for c in range(S5_CHUNKS):
        acc = jnp.zeros((tm, S5_CHUNK_IN), F32)
        for l in range(S5_SLABS):
            xr_c = xrs[l, c * pitch:c * pitch + tm, :].astype(BF16)
            xi_c = xis[l, c * pitch:c * pitch + tm, :].astype(BF16)
            acc = acc + jnp.dot(xr_c, cr_ref[c, l * LANES:(l + 1) * LANES, :], preferred_element_type=F32)
            acc = acc + jnp.dot(xi_c, ci_ref[c, l * LANES:(l + 1) * LANES, :], preferred_element_type=F32)
        cols = slice(c * S5_CHUNK_IN, (c + 1) * S5_CHUNK_IN)
        y = acc + d_ref[:, cols] * u_ref[:, cols].astype(F32)
        o_ref[:, cols] = jax.nn.gelu(y).astype(o_ref.dtype)


def _s5_operands(lam_re, lam_im, log_dt, b_re, b_im, c_re, c_im, d_skip):
    g, p = lam_re.shape
    i_dim = b_re.shape[-1]
    dt = jnp.exp(log_dt)[:, None]
    mag = jnp.exp(lam_re * dt)
    lbr = mag * jnp.cos(lam_im * dt)
    lbi = mag * jnp.sin(lam_im * dt)
    nr, ni = lbr - 1.0, lbi
    den = lam_re * lam_re + lam_im * lam_im
    cfr = (nr * lam_re + ni * lam_im) / den
    cfi = (ni * lam_re - nr * lam_im) / den
    bbr = cfr[..., None] * b_re - cfi[..., None] * b_im
    bbi = cfr[..., None] * b_im + cfi[..., None] * b_re
    gpc = g // S5_CHUNKS
    eye = jnp.eye(gpc, dtype=F32)

    def pack_b(m):
        m = m.reshape(S5_CHUNKS, gpc, p, i_dim)
        return jnp.einsum("cgpi,gh->cgihp", m, eye).reshape(S5_CHUNKS, gpc * i_dim, gpc * p)

    def pack_c(m):
        m = m.reshape(S5_CHUNKS, gpc, i_dim, p)
        return jnp.einsum("cgip,gh->cgphi", m, eye).reshape(S5_CHUNKS, gpc * p, gpc * i_dim)

    return (pack_b(bbr).astype(BF16), pack_b(bbi).astype(BF16),
            pack_c(c_re).astype(BF16), pack_c(-c_im).astype(BF16),
            lbr.reshape(S5_CHUNKS, gpc * p), lbi.reshape(S5_CHUNKS, gpc * p),
            d_skip.reshape(1, g * i_dim))


def _s5(proj, ops, bsz, seq, width):
    br, bi, cr, ci, lamr, lami, dsk = ops
    tm = _pick(seq, 256)
    pitch = tm + S5_PITCH_PAD
    nt = seq // tm
    kern = functools.partial(_s5_kernel, tm=tm, pitch=pitch)
    full = lambda a: pl.BlockSpec(a.shape, lambda b, i: (0,) * a.ndim)
    scr = pltpu.VMEM((S5_SLABS, S5_CHUNKS * pitch, LANES), F32)
    return pl.pallas_call(
        kern,
        grid=(bsz, nt),
        in_specs=[pl.BlockSpec((tm, width), lambda b, i: (b * nt + i, 0)),
                  full(br), full(bi), full(cr), full(ci), full(lamr), full(lami), full(dsk)],
        out_specs=pl.BlockSpec((tm, width), lambda b, i: (b * nt + i, 0)),
        out_shape=jax.ShapeDtypeStruct((bsz * seq, width), BF16),
        scratch_shapes=[scr, scr, scr, scr,
                        pltpu.VMEM((S5_CHUNKS, S5_CHUNK_STATE), F32),
                        pltpu.VMEM((S5_CHUNKS, S5_CHUNK_STATE), F32)],
        compiler_params=_params(("parallel", "arbitrary")),
        name="s5",
    )(proj, br, bi, cr, ci, lamr, lami, dsk)


def _attn_kernel(q_ref, k_ref, v_ref, tri_ref, o_ref, acc_ref, carry_ref, *, tb):
    qi = pl.program_id(2)
    q = q_ref[...]
    acc_ref[...] = jnp.zeros_like(acc_ref)
    carry_ref[...] = jnp.zeros_like(carry_ref)

    def tile(kb, diagonal):
        ks = pl.multiple_of(kb * tb, tb)
        k = k_ref[pl.ds(ks, tb), :]
        v = v_ref[pl.ds(ks, tb), :]
        z = lax.dot_general(q, k, (((1,), (1,)), ((), ())), preferred_element_type=F32)
        soft = jnp.log(1.0 + jnp.exp(-jnp.abs(z)))
        log_fail = -(jnp.maximum(z, 0.0) + soft)
        log_hit = jnp.minimum(z, 0.0) - soft
        if diagonal:
            row = lax.broadcasted_iota(jnp.int32, (tb, tb), 0)
            col = lax.broadcasted_iota(jnp.int32, (tb, tb), 1)
            causal = col < row
            log_fail = jnp.where(causal, log_fail, 0.0)
        sums = jnp.dot(log_fail.astype(BF16), tri_ref[...], preferred_element_type=F32)
        carry = carry_ref[...]
        log_w = log_hit + sums[:, :tb] + jnp.concatenate([carry] * (tb // LANES), axis=1)
        w = jnp.exp(log_w)
        if diagonal:
            w = jnp.where(causal, w, 0.0)
        acc_ref[...] += jnp.dot(w.astype(BF16), v, preferred_element_type=F32)
        carry_ref[...] = carry + sums[:, tb:]

    tile(qi, True)

    def body(it, _):
        tile(qi - 1 - it, False)
        return 0

    lax.fori_loop(0, qi, body, 0)
    o_ref[...] = acc_ref[...].astype(o_ref.dtype)


def _attn(proj, bsz, seq, q_col, k_col, v_col):
    tb = _pick(seq, 256)
    nq = seq // tb
    r = lax.broadcasted_iota(jnp.int32, (tb, tb + LANES), 0)
    c = lax.broadcasted_iota(jnp.int32, (tb, tb + LANES), 1)
    tri = jnp.where(jnp.logical_or(r > c, c >= tb), 1.0, 0.0).astype(BF16)
    kern = functools.partial(_attn_kernel, tb=tb)
    return pl.pallas_call(
        kern,
        grid=(bsz, SB_HEADS, nq),
        in_specs=[
            pl.BlockSpec((tb, SB_HEAD_DIM), lambda b, h, i: (b * nq + i, q_col + h)),
            pl.BlockSpec((seq, SB_HEAD_DIM), lambda b, h, i: (b, k_col + h)),
            pl.BlockSpec((seq, SB_HEAD_DIM), lambda b, h, i: (b, v_col + h)),
            pl.BlockSpec((tb, tb + LANES), lambda b, h, i: (0, 0)),
        ],
        out_specs=pl.BlockSpec((tb, SB_HEAD_DIM), lambda b, h, i: (b * nq + i, h)),
        out_shape=jax.ShapeDtypeStruct((bsz * seq, SB_HEADS * SB_HEAD_DIM), BF16),
        scratch_shapes=[pltpu.VMEM((tb, SB_HEAD_DIM), F32), pltpu.VMEM((tb, LANES), F32)],
        compiler_params=_params(("parallel", "parallel", "arbitrary")),
        name="attn",
    )(proj, proj, proj, tri)


def _mix_kernel(y_ref, att_ref, gs_ref, ga_ref, x_ref, mod_ref, wglu_ref, wup_ref, wout_ref, g2_ref,
                x1_ref, h2_ref, h2t_ref, *, d):
    yg = jnp.dot(y_ref[...], wglu_ref[...], preferred_element_type=F32)
    ssm = yg[:, :d] * jax.nn.sigmoid(yg[:, d:])
    up = jnp.dot(att_ref[...], wup_ref[...], preferred_element_type=F32)
    merged = (jax.nn.sigmoid(gs_ref[...].astype(F32)) * ssm
              + jax.nn.sigmoid(ga_ref[...].astype(F32)) * up)
    o = jnp.dot(merged.astype(BF16), wout_ref[...], preferred_element_type=F32)
    x1 = x_ref[...] + mod_ref[0, 2:3, :] * o
    x1_ref[...] = x1
    ms = jnp.mean(x1 * x1, axis=-1, keepdims=True)
    h2 = x1 * lax.rsqrt(ms + EPS) * g2_ref[...]
    h2 = h2 * (1.0 + mod_ref[0, 4:5, :]) + mod_ref[0, 3:4, :]
    h2_ref[...] = h2.astype(BF16)
    h2t_ref[...] = h2.T.astype(BF16)


def _mix(ygelu, att, proj, x2, mod3, wglu, wup, wout, g2, seq, gs_col, ga_col):
    t, d = x2.shape
    w = ygelu.shape[1]
    tm = _pick(seq, 256)
    spb = seq // tm
    kern = functools.partial(_mix_kernel, d=d)
    const = lambda a: pl.BlockSpec(a.shape, lambda i: (0,) * a.ndim)
    return pl.pallas_call(
        kern,
        grid=(t // tm,),
        in_specs=[
            pl.BlockSpec((tm, w), lambda i: (i, 0)),
            pl.BlockSpec((tm, w), lambda i: (i, 0)),
            pl.BlockSpec((tm, d), lambda i: (i, gs_col)),
            pl.BlockSpec((tm, d), lambda i: (i, ga_col)),
            pl.BlockSpec((tm, d), lambda i: (i, 0)),
            pl.BlockSpec((1, N_ADA, d), lambda i: (i // spb, 0, 0)),
            const(wglu), const(wup), const(wout), const(g2),
        ],
        out_specs=[
            pl.BlockSpec((tm, d), lambda i: (i, 0)),
            pl.BlockSpec((tm, d), lambda i: (i, 0)),
            pl.BlockSpec((d, tm), lambda i: (0, i)),
        ],
        out_shape=[jax.ShapeDtypeStruct((t, d), F32),
                   jax.ShapeDtypeStruct((t, d), BF16),
                   jax.ShapeDtypeStruct((d, t), BF16)],
        compiler_params=_params(("parallel",)),
        name="mix",
    )(ygelu, att, proj, proj, x2, mod3, wglu, wup, wout, g2)


PEER_RANKS = PEER_TOPK + 1
PEER_RANK_ROWS = 24


def _candidate_tiles(v1_scr, v2_scr, tm):
    tiles = []
    row8 = lax.broadcasted_iota(jnp.int32, (SUBLANES, tm), 0)
    for b in range(PEER_RANKS):
        n_a = PEER_RANKS // (b + 1)
        v2b = v2_scr[b:b + 1, :]
        for base in range(0, n_a, SUBLANES):
            t = v1_scr[base:base + SUBLANES, :] + v2b
            if n_a - base < SUBLANES:
                t = jnp.where(row8 < (n_a - base), t, NEG_INF)
            tiles.append(t)
    return tiles


def _route_kernel(h2_ref, wq_ref, k1_ref, k2_ref, c1_ref, w1_ref, s2_ref, e2_ref,
                  s_scr, v1_scr, v2_scr, *, tm):
    q = jnp.dot(h2_ref[...], wq_ref[...], preferred_element_type=F32).astype(k1_ref.dtype)
    v1_scr[...] = jnp.full(v1_scr.shape, NEG_INF, F32)
    v2_scr[...] = jnp.full(v2_scr.shape, NEG_INF, F32)
    for h in range(PEER_HEADS):
        for half, (k_ref, v_scr) in enumerate(((k1_ref, v1_scr), (k2_ref, v2_scr))):
            col = (2 * h + half) * PEER_HALF
            qh = q[:, col:col + PEER_HALF]
            s = lax.dot_general(k_ref[...], qh, (((1,), (1,)), ((), ())), preferred_element_type=F32)
            s_scr[half] = s
            for a in range(PEER_RANKS):
                m = jnp.max(s, axis=0, keepdims=True)
                v_scr[a:a + 1, :] = m
                s = jnp.where(s >= m, NEG_INF, s)
        tiles = _candidate_tiles(v1_scr, v2_scr, tm)
        work = list(tiles)
        ranked = []
        for a in range(PEER_RANKS):
            m = work[0]
            for t in work[1:]:
                m = jnp.maximum(m, t)
            m = jnp.max(m, axis=0, keepdims=True)
            ranked.append(m)
            work = [jnp.where(t >= m, NEG_INF, t) for t in work]
        top = ranked[0]
        thr = 0.5 * (ranked[PEER_TOPK - 1] + ranked[PEER_TOPK])
        zsum = jnp.zeros((SUBLANES, tm), F32)
        for t in tiles:
            zsum = zsum + jnp.where(t >= thr, jnp.exp(t - top), 0.0)
        inv_z = 1.0 / jnp.sum(zsum, axis=0, keepdims=True)
        s1 = s_scr[0]
        s2 = s_scr[1]
        c1_ref[h] = thr - s1
        w1_ref[h] = jnp.exp(s1 - v1_scr[0:1, :]) * inv_z
        s2_ref[h] = s2
        e2_ref[h] = jnp.exp(s2 - v2_scr[0:1, :])


def _route(h2, wq, k1, k2):
    t, d = h2.shape
    tm = _pick(t, 256)
    kern = functools.partial(_route_kernel, tm=tm)
    const = lambda a: pl.BlockSpec(a.shape, lambda i: (0,) * a.ndim)
    out_spec = pl.BlockSpec((PEER_HEADS, PEER_N_KEYS, tm), lambda i: (0, 0, i))
    out_sds = jax.ShapeDtypeStruct((PEER_HEADS, PEER_N_KEYS, t), F32)
    return pl.pallas_call(
        kern,
        grid=(t // tm,),
        in_specs=[pl.BlockSpec((tm, d), lambda i: (i, 0)), const(wq), const(k1), const(k2)],
        out_specs=[out_spec] * 4,
        out_shape=[out_sds] * 4,
        scratch_shapes=[pltpu.VMEM((2, PEER_N_KEYS, tm), F32),
                        pltpu.VMEM((PEER_RANK_ROWS, tm), F32),
                        pltpu.VMEM((PEER_RANK_ROWS, tm), F32)],
        compiler_params=_params(("parallel",)),
        name="route",
    )(h2, wq, k1, k2)


def _experts_kernel(u_ref, h2t_ref, vt_ref, c1_ref, w1_ref, s2_ref, e2_ref, x1_ref, mod_ref, o_ref,
                    acc_ref, p_scr, *, te, tm):
    e = pl.program_id(1)

    @pl.when(e == 0)
    def _():
        acc_ref[...] = jnp.zeros_like(acc_ref)

    act = jax.nn.gelu(jnp.dot(u_ref[...], h2t_ref[...], preferred_element_type=F32))
    n_i = te // PEER_N_KEYS
    for ii in range(n_i):
        i = e * n_i + ii
        gate = jnp.zeros((PEER_N_KEYS, tm), F32)
        for h in range(PEER_HEADS):
            c = c1_ref[h, pl.ds(i, 1), :]
            w1 = w1_ref[h, pl.ds(i, 1), :]
            gate = gate + jnp.where(s2_ref[h] >= c, e2_ref[h] * w1, 0.0)
        rows = slice(ii * PEER_N_KEYS, (ii + 1) * PEER_N_KEYS)
        p_scr[rows, :] = (gate * act[rows, :]).astype(BF16)
    acc_ref[...] += jnp.dot(vt_ref[...], p_scr[...], preferred_element_type=F32)

    @pl.when(e == pl.num_programs(1) - 1)
    def _():
        o_ref[...] = x1_ref[...] + mod_ref[0, 5:6, :] * acc_ref[...].T


def _experts(u_tab, h2t, vt_tab, route_out, x1, mod3, seq):
    n_e, d = u_tab.shape
    t = h2t.shape[1]
    tm = _pick(seq, 512)
    te = _pick(n_e, 512)
    spb = seq // tm
    kern = functools.partial(_experts_kernel, te=te, tm=tm)
    rspec = pl.BlockSpec((PEER_HEADS, PEER_N_KEYS, tm), lambda i, e: (0, 0, i))
    return pl.pallas_call(
        kern,
        grid=(t // tm, n_e // te),
        in_specs=[
            pl.BlockSpec((te, d), lambda i, e: (e, 0)),
            pl.BlockSpec((d, tm), lambda i, e: (0, i)),
            pl.BlockSpec((d, te), lambda i, e: (0, e)),
            rspec, rspec, rspec, rspec,
            pl.BlockSpec((tm, d), lambda i, e: (i, 0)),
            pl.BlockSpec((1, N_ADA, d), lambda i, e: (i // spb, 0, 0)),
        ],
        out_specs=pl.BlockSpec((tm, d), lambda i, e: (i, 0)),
        out_shape=jax.ShapeDtypeStruct((t, d), F32),
        scratch_shapes=[pltpu.VMEM((d, tm), F32), pltpu.VMEM((te, tm), BF16)],
        compiler_params=_params(("parallel", "arbitrary")),
        name="experts",
    )(u_tab, h2t, vt_tab, *route_out, x1, mod3)


def kernel(x, c, w_ada, b_ada, norm1_g, w_in, lam_re, lam_im, log_dt, ssm_b_re, ssm_b_im, ssm_c_re, ssm_c_im, ssm_d, w_glu, q_norm_g, k_norm_g, w_att_up, w_out, norm2_g, peer_wq, peer_k1, peer_k2, peer_u, peer_v):
    bsz, seq, d = x.shape
    depth = w_ada.shape[0]
    ssm_width = ssm_d.shape[1] * ssm_d.shape[2]
    sb_width = SB_HEADS * SB_HEAD_DIM
    assert ssm_width == sb_width and d % sb_width == 0
    assert ssm_d.shape[1] % S5_CHUNKS == 0 and ssm_d.shape[2] == SSM_GROUP and lam_re.shape[2] == SSM_STATE
    q_col = sb_width // SB_HEAD_DIM
    k_col = 2 * q_col
    v_col = 3 * q_col
    gate_base = 4 * sb_width
    assert gate_base % d == 0
    gs_col = gate_base // d
    ga_col = gs_col + 1

    x2 = x.reshape(bsz * seq, d)
    c_pad = jnp.zeros((SUBLANES, d), F32).at[:bsz].set(c)
    for l in range(depth):
        mod = _ada(c_pad, w_ada[l], b_ada[l].reshape(1, -1))
        mod3 = mod.reshape(SUBLANES, N_ADA, d)
        proj = _inproj(x2, mod3, norm1_g[l].reshape(1, d), w_in[l].astype(BF16),
                       q_norm_g[l].reshape(1, -1), k_norm_g[l].reshape(1, -1), seq, sb_width)
        s5_ops = _s5_operands(lam_re[l], lam_im[l], log_dt[l], ssm_b_re[l], ssm_b_im[l],
                              ssm_c_re[l], ssm_c_im[l], ssm_d[l])
        ygelu = _s5(proj, s5_ops, bsz, seq, ssm_width)
        att = _attn(proj, bsz, seq, q_col, k_col, v_col)
        x1, h2, h2t = _mix(ygelu, att, proj, x2, mod3, w_glu[l].astype(BF16), w_att_up[l].astype(BF16),
                           w_out[l].astype(BF16), norm2_g[l].reshape(1, d), seq, gs_col, ga_col)
        route_out = _route(h2, peer_wq[l].astype(BF16), peer_k1[l].astype(BF16), peer_k2[l].astype(BF16))
        x2 = _experts(peer_u[l].astype(BF16), h2t, peer_v[l].T.astype(BF16), route_out, x1, mod3, seq)
    return x2.reshape(bsz, seq, d)
```

```python
import functools
import math

import jax
import jax.numpy as jnp
from jax import lax
from jax.experimental import pallas as pl
from jax.experimental.pallas import tpu as pltpu

F32 = jnp.float32
BF16 = jnp.bfloat16

LANES = 128
SUBLANES = 8
VMEM_LIMIT = 56 * 1024 * 1024

SSM_GROUP = 16
SSM_STATE = 64
SB_HEADS = 8
SB_HEAD_DIM = 128
PEER_HEADS = 8
PEER_N_KEYS = 128
PEER_TOPK = 16
PEER_HALF = 128
N_ADA = 6
EPS = 1e-6
NEG_INF = float("-inf")
LOG2E = math.log2(math.e)


def _params(sem):
    return pltpu.CompilerParams(dimension_semantics=sem, vmem_limit_bytes=VMEM_LIMIT)


GELU_C = -2.0 * math.sqrt(2.0 / math.pi) * LOG2E


def _gelu(x):
    return x / (1.0 + jnp.exp2(x * (GELU_C + (GELU_C * 0.044715) * (x * x))))


def _pick(n, pref):
    t = min(n, pref)
    while n % t:
        t //= 2
    return t


def _ada_kernel(c_ref, w_ref, b_ref, o_ref):
    c = c_ref[...]
    c = c * jax.nn.sigmoid(c)
    o_ref[...] = jnp.dot(c, w_ref[...], preferred_element_type=F32) + b_ref[...]


def _ada(c_pad, w, b):
    rows, d = c_pad.shape
    n = w.shape[1]
    tn = _pick(n, 1024)
    return pl.pallas_call(
        _ada_kernel,
        grid=(n // tn,),
        in_specs=[
            pl.BlockSpec((rows, d), lambda j: (0, 0)),
            pl.BlockSpec((d, tn), lambda j: (0, j)),
            pl.BlockSpec((1, tn), lambda j: (0, j)),
        ],
        out_specs=pl.BlockSpec((rows, tn), lambda j: (0, j)),
        out_shape=jax.ShapeDtypeStruct((rows, n), F32),
        compiler_params=_params(("arbitrary",)),
        name="ada",
    )(c_pad, w, b)


def _head_rms(acc, g_ref, o_ref, scale):
    for h in range(acc.shape[1] // SB_HEAD_DIM):
        blk = acc[:, h * SB_HEAD_DIM:(h + 1) * SB_HEAD_DIM]
        ms = jnp.mean(blk * blk, axis=-1, keepdims=True)
        y = blk * lax.rsqrt(ms + EPS) * g_ref[...]
        if scale != 1.0:
            y = y * scale
        o_ref[:, h * SB_HEAD_DIM:(h + 1) * SB_HEAD_DIM] = y.astype(o_ref.dtype)


def _inproj_kernel(x_ref, mod_ref, g_ref, w_ref, qg_ref, kg_ref, o_ref, h_scr, *, q_blk, k_blk):
    j = pl.program_id(1)

    @pl.when(j == 0)
    def _():
        x = x_ref[...]
        ms = jnp.mean(x * x, axis=-1, keepdims=True)
        y = x * lax.rsqrt(ms + EPS) * g_ref[...]
        h = y * (1.0 + mod_ref[0, 1:2, :]) + mod_ref[0, 0:1, :]
        h_scr[...] = h.astype(BF16)

    acc = jnp.dot(h_scr[...], w_ref[...], preferred_element_type=F32)

    @pl.when(j == q_blk)
    def _():
        _head_rms(acc, qg_ref, o_ref, SB_HEAD_DIM ** -0.5 * LOG2E)

    @pl.when(j == k_blk)
    def _():
        _head_rms(acc, kg_ref, o_ref, 1.0)

    @pl.when(jnp.logical_and(j != q_blk, j != k_blk))
    def _():
        o_ref[...] = acc.astype(o_ref.dtype)


def _inproj(x2, mod3, g, w, qg, kg, seq, sb_width):
    t, d = x2.shape
    n = w.shape[1]
    tm = _pick(seq, 512)
    tn = sb_width
    steps_per_batch = seq // tm
    kern = functools.partial(_inproj_kernel, q_blk=1, k_blk=2)
    return pl.pallas_call(
        kern,
        grid=(t // tm, n // tn),
        in_specs=[
            pl.BlockSpec((tm, d), lambda i, j: (i, 0)),
            pl.BlockSpec((1, N_ADA, d), lambda i, j: (i // steps_per_batch, 0, 0)),
            pl.BlockSpec((1, d), lambda i, j: (0, 0)),
            pl.BlockSpec((d, tn), lambda i, j: (0, j)),
            pl.BlockSpec((1, SB_HEAD_DIM), lambda i, j: (0, 0)),
            pl.BlockSpec((1, SB_HEAD_DIM), lambda i, j: (0, 0)),
        ],
        out_specs=pl.BlockSpec((tm, tn), lambda i, j: (i, j)),
        out_shape=jax.ShapeDtypeStruct((t, n), BF16),
        scratch_shapes=[pltpu.VMEM((tm, d), BF16)],
        compiler_params=_params(("parallel", "arbitrary")),
        name="inproj",
    )(x2, mod3, g, w, qg, kg)


S5_CHUNKS = 8
S5_CHUNK_IN = LANES
S5_CHUNK_STATE = 8 * SSM_STATE
S5_SLABS = S5_CHUNK_STATE // LANES
S5_PITCH_PAD = SUBLANES


def _s5_kernel(u_ref, br_ref, bi_ref, cr_ref, ci_ref, lamr_ref, lami_ref, d_ref, o_ref,
               bur, bui, xrs, xis, sr, si, *, tm, pitch):
    @pl.when(pl.program_id(1) == 0)
    def _():
        sr[...] = jnp.zeros_like(sr)
        si[...] = jnp.zeros_like(si)

    for c in range(S5_CHUNKS):
        uc = u_ref[:, c * S5_CHUNK_IN:(c + 1) * S5_CHUNK_IN]
        r = jnp.dot(uc, br_ref[c], preferred_element_type=F32)
        im = jnp.dot(uc, bi_ref[c], preferred_element_type=F32)
        for l in range(S5_SLABS):
            bur[l, c * pitch:c * pitch + tm, :] = r[:, l * LANES:(l + 1) * LANES]
            bui[l, c * pitch:c * pitch + tm, :] = im[:, l * LANES:(l + 1) * LANES]

    lamr = [lamr_ref[:, l * LANES:(l + 1) * LANES] for l in range(S5_SLABS)]
    lami = [lami_ref[:, l * LANES:(l + 1) * LANES] for l in range(S5_SLABS)]

    def body(t, carry):
        xr, xi = carry
        nxr, nxi = [], []
        for l in range(S5_SLABS):
            rows = pl.ds(t, S5_CHUNKS, stride=pitch)
            b_r = bur[l, rows, :]
            b_i = bui[l, rows, :]
            n_r = lamr[l] * xr[l] - lami[l] * xi[l] + b_r
            n_i = lamr[l] * xi[l] + lami[l] * xr[l] + b_i
            xrs[l, rows, :] = n_r
            xis[l, rows, :] = n_i
            nxr.append(n_r)
            nxi.append(n_i)
        return tuple(nxr), tuple(nxi)

    init = (tuple(sr[:, l * LANES:(l + 1) * LANES] for l in range(S5_SLABS)),
            tuple(si[:, l * LANES:(l + 1) * LANES] for l in range(S5_SLABS)))
    xr, xi = lax.fori_loop(0, tm, body, init, unroll=8)
    for l in range(S5_SLABS):
        sr[:, l * LANES:(l + 1) * LANES] = xr[l]
        si[:, l * LANES:(l + 1) * LANES] = xi[l]

    for c in range(S5_CHUNKS):
        acc = jnp.zeros((tm, S5_CHUNK_IN), F32)
        for l in range(S5_SLABS):
            xr_c = xrs[l, c * pitch:c * pitch + tm, :].astype(BF16)
            xi_c = xis[l, c * pitch:c * pitch + tm, :].astype(BF16)
            acc = acc + jnp.dot(xr_c, cr_ref[c, l * LANES:(l + 1) * LANES, :], preferred_element_type=F32)
            acc = acc + jnp.dot(xi_c, ci_ref[c, l * LANES:(l + 1) * LANES, :], preferred_element_type=F32)
        cols = slice(c * S5_CHUNK_IN, (c + 1) * S5_CHUNK_IN)
        y = acc + d_ref[:, cols] * u_ref[:, cols].astype(F32)
        o_ref[:, cols] = _gelu(y).astype(o_ref.dtype)


def _s5_operands(lam_re, lam_im, log_dt, b_re, b_im, c_re, c_im, d_skip):
    g, p = lam_re.shape
    i_dim = b_re.shape[-1]
    dt = jnp.exp(log_dt)[:, None]
    mag = jnp.exp(lam_re * dt)
    lbr = mag * jnp.cos(lam_im * dt)
    lbi = mag * jnp.sin(lam_im * dt)
    nr, ni = lbr - 1.0, lbi
    den = lam_re * lam_re + lam_im * lam_im
    cfr = (nr * lam_re + ni * lam_im) / den
    cfi = (ni * lam_re - nr * lam_im) / den
    bbr = cfr[..., None] * b_re - cfi[..., None] * b_im
    bbi = cfr[..., None] * b_im + cfi[..., None] * b_re
    gpc = g // S5_CHUNKS
    eye = jnp.eye(gpc, dtype=F32)

    def pack_b(m):
        m = m.reshape(S5_CHUNKS, gpc, p, i_dim)
        return jnp.einsum("cgpi,gh->cgihp", m, eye).reshape(S5_CHUNKS, gpc * i_dim, gpc * p)

    def pack_c(m):
        m = m.reshape(S5_CHUNKS, gpc, i_dim, p)
        return jnp.einsum("cgip,gh->cgphi", m, eye).reshape(S5_CHUNKS, gpc * p, gpc * i_dim)

    return (pack_b(bbr).astype(BF16), pack_b(bbi).astype(BF16),
            pack_c(c_re).astype(BF16), pack_c(-c_im).astype(BF16),
            lbr.reshape(S5_CHUNKS, gpc * p), lbi.reshape(S5_CHUNKS, gpc * p),
            d_skip.reshape(1, g * i_dim))


def _s5(proj, ops, bsz, seq, width):
    br, bi, cr, ci, lamr, lami, dsk = ops
    tm = _pick(seq, 256)
    pitch = tm + S5_PITCH_PAD
    nt = seq // tm
    kern = functools.partial(_s5_kernel, tm=tm, pitch=pitch)
    full = lambda a: pl.BlockSpec(a.shape, lambda b, i: (0,) * a.ndim)
    scr = pltpu.VMEM((S5_SLABS, S5_CHUNKS * pitch, LANES), F32)
    return pl.pallas_call(
        kern,
        grid=(bsz, nt),
        in_specs=[pl.BlockSpec((tm, width), lambda b, i: (b * nt + i, 0)),
                  full(br), full(bi), full(cr), full(ci), full(lamr), full(lami), full(dsk)],
        out_specs=pl.BlockSpec((tm, width), lambda b, i: (b * nt + i, 0)),
        out_shape=jax.ShapeDtypeStruct((bsz * seq, width), BF16),
        scratch_shapes=[scr, scr, scr, scr,
                        pltpu.VMEM((S5_CHUNKS, S5_CHUNK_STATE), F32),
                        pltpu.VMEM((S5_CHUNKS, S5_CHUNK_STATE), F32)],
        compiler_params=_params(("parallel", "arbitrary")),
        name="s5",
    )(proj, br, bi, cr, ci, lamr, lami, dsk)


ATTN_GROUP = 4
ATTN_HEADS = 2


def _neg_abs(z):
    bits = lax.bitcast_convert_type(z, jnp.uint32) | jnp.uint32(0x80000000)
    return lax.bitcast_convert_type(bits, F32)


def _attn_kernel(q_ref, k_ref, v_ref, tri_ref, o_ref, acc_ref, carry_ref, *, tb):
    qi = pl.program_id(2)
    acc_ref[...] = jnp.zeros_like(acc_ref)
    carry_ref[...] = jnp.zeros_like(carry_ref)

    def group(kb_lo, n, diagonal):
        ks = pl.multiple_of(kb_lo * tb, tb)
        for hh in range(ATTN_HEADS):
            cols = slice(hh * SB_HEAD_DIM, (hh + 1) * SB_HEAD_DIM)
            q = q_ref[:, cols]
            k = k_ref[pl.ds(ks, n * tb), cols]
            v = v_ref[pl.ds(ks, n * tb), cols]
            carry = carry_ref[hh]
            ws = [None] * n
            for g in reversed(range(n)):
                z = lax.dot_general(q, k[g * tb:(g + 1) * tb, :], (((1,), (1,)), ((), ())),
                                    preferred_element_type=F32)
                sp = jnp.maximum(z, 0.0) + jnp.log(1.0 + jnp.exp2(_neg_abs(z))) * LOG2E
                masked = diagonal and g == n - 1
                if masked:
                    row = lax.broadcasted_iota(jnp.int32, (tb, tb), 0)
                    col = lax.broadcasted_iota(jnp.int32, (tb, tb), 1)
                    causal = col < row
                    sp_sum = jnp.where(causal, sp, 0.0)
                else:
                    sp_sum = sp
                cum = jnp.dot(sp_sum.astype(BF16), tri_ref[...], preferred_element_type=F32)
                w = jnp.exp2(z - sp - cum - carry)
                if masked:
                    w = jnp.where(causal, w, 0.0)
                ws[g] = w.astype(BF16)
                carry = carry + jnp.sum(sp_sum, axis=1, keepdims=True)
            w_all = ws[0] if n == 1 else jnp.concatenate(ws, axis=1)
            acc_ref[:, cols] += jnp.dot(w_all, v, preferred_element_type=F32)
            carry_ref[hh] = carry

    group(qi, 1, True)
    done = jnp.int32(0)
    size = 1
    while size < ATTN_GROUP:
        take = (qi & size) != 0

        @pl.when(take)
        def _(done=done, size=size):
            group(qi - done - size, size, False)

        done = done + jnp.where(take, size, 0)
        size *= 2

    def multi(it, _):
        group(qi - done - (it + 1) * ATTN_GROUP, ATTN_GROUP, False)
        return 0

    lax.fori_loop(0, qi // ATTN_GROUP, multi, 0)
    o_ref[...] = acc_ref[...].astype(o_ref.dtype)


def _attn(proj, bsz, seq, q_col, k_col, v_col):
    tb = _pick(seq, 256)
    nq = seq // tb
    r = lax.broadcasted_iota(jnp.int32, (tb, tb), 0)
    c = lax.broadcasted_iota(jnp.int32, (tb, tb), 1)
    tri = jnp.where(r > c, 1.0, 0.0).astype(BF16)
    kern = functools.partial(_attn_kernel, tb=tb)
    hw = ATTN_HEADS * SB_HEAD_DIM
    assert q_col % ATTN_HEADS == 0 and SB_HEADS % ATTN_HEADS == 0
    qc, kc, vc = q_col // ATTN_HEADS, k_col // ATTN_HEADS, v_col // ATTN_HEADS
    return pl.pallas_call(
        kern,
        grid=(bsz, SB_HEADS // ATTN_HEADS, nq),
        in_specs=[
            pl.BlockSpec((tb, hw), lambda b, h, i: (b * nq + i, qc + h)),
            pl.BlockSpec((seq, hw), lambda b, h, i: (b, kc + h)),
            pl.BlockSpec((seq, hw), lambda b, h, i: (b, vc + h)),
            pl.BlockSpec((tb, tb), lambda b, h, i: (0, 0)),
        ],
        out_specs=pl.BlockSpec((tb, hw), lambda b, h, i: (b * nq + i, h)),
        out_shape=jax.ShapeDtypeStruct((bsz * seq, SB_HEADS * SB_HEAD_DIM), BF16),
        scratch_shapes=[pltpu.VMEM((tb, hw), F32), pltpu.VMEM((ATTN_HEADS, tb, 1), F32)],
        compiler_params=_params(("parallel", "parallel", "arbitrary")),
        name="attn",
    )(proj, proj, proj, tri)


def _mix_kernel(y_ref, att_ref, gs_ref, ga_ref, x_ref, mod_ref, wglu_ref, wup_ref, wout_ref, g2_ref,
                x1_ref, h2_ref, h2t_ref, *, d):
    yg = jnp.dot(y_ref[...], wglu_ref[...], preferred_element_type=F32)
    ssm = yg[:, :d] * jax.nn.sigmoid(yg[:, d:])
    up = jnp.dot(att_ref[...], wup_ref[...], preferred_element_type=F32)
    merged = (jax.nn.sigmoid(gs_ref[...].astype(F32)) * ssm
              + jax.nn.sigmoid(ga_ref[...].astype(F32)) * up)
    o = jnp.dot(merged.astype(BF16), wout_ref[...], preferred_element_type=F32)
    x1 = x_ref[...] + mod_ref[0, 2:3, :] * o
    x1_ref[...] = x1
    ms = jnp.mean(x1 * x1, axis=-1, keepdims=True)
    h2 = x1 * lax.rsqrt(ms + EPS) * g2_ref[...]
    h2 = h2 * (1.0 + mod_ref[0, 4:5, :]) + mod_ref[0, 3:4, :]
    h2_ref[...] = h2.astype(BF16)
    h2t_ref[...] = h2.T.astype(BF16)


def _mix(ygelu, att, proj, x2, mod3, wglu, wup, wout, g2, seq, gs_col, ga_col):
    t, d = x2.shape
    w = ygelu.shape[1]
    tm = _pick(seq, 256)
    spb = seq // tm
    kern = functools.partial(_mix_kernel, d=d)
    const = lambda a: pl.BlockSpec(a.shape, lambda i: (0,) * a.ndim)
    return pl.pallas_call(
        kern,
        grid=(t // tm,),
        in_specs=[
            pl.BlockSpec((tm, w), lambda i: (i, 0)),
            pl.BlockSpec((tm, w), lambda i: (i, 0)),
            pl.BlockSpec((tm, d), lambda i: (i, gs_col)),
            pl.BlockSpec((tm, d), lambda i: (i, ga_col)),
            pl.BlockSpec((tm, d), lambda i: (i, 0)),
            pl.BlockSpec((1, N_ADA, d), lambda i: (i // spb, 0, 0)),
            const(wglu), const(wup), const(wout), const(g2),
        ],
        out_specs=[
            pl.BlockSpec((tm, d), lambda i: (i, 0)),
            pl.BlockSpec((tm, d), lambda i: (i, 0)),
            pl.BlockSpec((d, tm), lambda i: (0, i)),
        ],
        out_shape=[jax.ShapeDtypeStruct((t, d), F32),
                   jax.ShapeDtypeStruct((t, d), BF16),
                   jax.ShapeDtypeStruct((d, t), BF16)],
        compiler_params=_params(("parallel",)),
        name="mix",
    )(ygelu, att, proj, proj, x2, mod3, wglu, wup, wout, g2)


PEER_RANKS = PEER_TOPK + 1
PEER_RANK_ROWS = 24


def _candidate_tiles(v1_scr, v2_scr, tm):
    tiles = []
    row8 = lax.broadcasted_iota(jnp.int32, (SUBLANES, tm), 0)
    for b in range(PEER_RANKS):
        n_a = PEER_RANKS // (b + 1)
        v2b = v2_scr[b:b + 1, :]
        for base in range(0, n_a, SUBLANES):
            t = v1_scr[base:base + SUBLANES, :] + v2b
            if n_a - base < SUBLANES:
                t = jnp.where(row8 < (n_a - base), t, NEG_INF)
            tiles.append(t)
    return tiles


def _route_kernel(h2_ref, wq_ref, k1_ref, k2_ref, c1_ref, w1_ref, e2_ref,
                  s_scr, v1_scr, v2_scr, *, tm):
    q = jnp.dot(h2_ref[...], wq_ref[...], preferred_element_type=F32).astype(k1_ref.dtype)
    v1_scr[...] = jnp.full(v1_scr.shape, NEG_INF, F32)
    v2_scr[...] = jnp.full(v2_scr.shape, NEG_INF, F32)
    for h in range(PEER_HEADS):
        for half, (k_ref, v_scr) in enumerate(((k1_ref, v1_scr), (k2_ref, v2_scr))):
            col = (2 * h + half) * PEER_HALF
            qh = q[:, col:col + PEER_HALF]
            s = lax.dot_general(k_ref[...], qh, (((1,), (1,)), ((), ())), preferred_element_type=F32)
            s_scr[half] = s
            for a in range(PEER_RANKS):
                m = jnp.max(s, axis=0, keepdims=True)
                v_scr[a:a + 1, :] = m
                s = jnp.where(s >= m, NEG_INF, s)
        tiles = _candidate_tiles(v1_scr, v2_scr, tm)
        work = list(tiles)
        ranked = []
        for a in range(PEER_RANKS):
            m = work[0]
            for t in work[1:]:
                m = jnp.maximum(m, t)
            m = jnp.max(m, axis=0, keepdims=True)
            ranked.append(m)
            work = [jnp.where(t >= m, NEG_INF, t) for t in work]
        top = ranked[0]
        thr = 0.5 * (ranked[PEER_TOPK - 1] + ranked[PEER_TOPK])
        zsum = jnp.zeros((SUBLANES, tm), F32)
        for t in tiles:
            zsum = zsum + jnp.where(t >= thr, jnp.exp(t - top), 0.0)
        inv_z = 1.0 / jnp.sum(zsum, axis=0, keepdims=True)
        s1 = s_scr[0]
        s2 = s_scr[1]
        top2 = v2_scr[0:1, :]
        c1_ref[h] = jnp.exp(thr - s1 - top2)
        w1_ref[h] = jnp.exp(s1 - v1_scr[0:1, :]) * inv_z
        e2_ref[h] = jnp.exp(s2 - top2)


def _route(h2, wq, k1, k2):
    t, d = h2.shape
    tm = _pick(t, 256)
    kern = functools.partial(_route_kernel, tm=tm)
    const = lambda a: pl.BlockSpec(a.shape, lambda i: (0,) * a.ndim)
    out_spec = pl.BlockSpec((PEER_HEADS, PEER_N_KEYS, tm), lambda i: (0, 0, i))
    out_sds = jax.ShapeDtypeStruct((PEER_HEADS, PEER_N_KEYS, t), F32)
    return pl.pallas_call(
        kern,
        grid=(t // tm,),
        in_specs=[pl.BlockSpec((tm, d), lambda i: (i, 0)), const(wq), const(k1), const(k2)],
        out_specs=[out_spec] * 3,
        out_shape=[out_sds] * 3,
        scratch_shapes=[pltpu.VMEM((2, PEER_N_KEYS, tm), F32),
                        pltpu.VMEM((PEER_RANK_ROWS, tm), F32),
                        pltpu.VMEM((PEER_RANK_ROWS, tm), F32)],
        compiler_params=_params(("parallel",)),
        name="route",
    )(h2, wq, k1, k2)


def _expert_scores(u_ref, h2t_ref, act_w):
    act_w[...] = jnp.dot(u_ref[...], h2t_ref[...], preferred_element_type=F32)


def _expert_values(i0, act_r, vt_ref, c1_ref, w1_ref, e2_ref, acc_ref, *, te, tm):
    ps = []
    for ii in range(te // PEER_N_KEYS):
        i = i0 + ii
        rows = slice(ii * PEER_N_KEYS, (ii + 1) * PEER_N_KEYS)
        c_rows = [c1_ref[h, pl.ds(i, 1), :] for h in range(PEER_HEADS)]
        w_rows = [w1_ref[h, pl.ds(i, 1), :] for h in range(PEER_HEADS)]
        chunks = []
        for tc in range(tm // LANES):
            cols = slice(tc * LANES, (tc + 1) * LANES)
            gate = None
            for h in range(PEER_HEADS):
                e2 = e2_ref[h, :, cols]
                term = jnp.where(e2 >= c_rows[h][:, cols], e2 * w_rows[h][:, cols], 0.0)
                gate = term if gate is None else gate + term
            chunks.append((gate * _gelu(act_r[rows, cols])).astype(BF16))
        ps.append(jnp.concatenate(chunks, axis=1))
    p = ps[0] if len(ps) == 1 else jnp.concatenate(ps, axis=0)
    acc_ref[...] += jnp.dot(vt_ref[...], p, preferred_element_type=F32)


def _experts_kernel(u_ref, h2t_ref, vt_ref, c1_ref, w1_ref, e2_ref, x1_ref, mod_ref, o_ref,
                    acc_ref, act0, act1, *, te, tm, ne):
    e = pl.program_id(1)
    n_i = te // PEER_N_KEYS
    slots = (act0, act1)
    values = functools.partial(_expert_values, vt_ref=vt_ref, c1_ref=c1_ref, w1_ref=w1_ref, e2_ref=e2_ref,
                               acc_ref=acc_ref, te=te, tm=tm)

    @pl.when(e == 0)
    def _():
        acc_ref[...] = jnp.zeros_like(acc_ref)
        _expert_scores(u_ref, h2t_ref, slots[0])

    for parity in range(2):
        @pl.when(jnp.logical_and(jnp.logical_and(e > 0, e < ne), lax.rem(e, 2) == parity))
        def _(parity=parity):
            _expert_scores(u_ref, h2t_ref, slots[parity])
            values((e - 1) * n_i, slots[1 - parity])

    @pl.when(e == ne)
    def _():
        values((ne - 1) * n_i, slots[(ne - 1) % 2])
        o_ref[...] = x1_ref[...] + mod_ref[0, 5:6, :] * acc_ref[...].T


def _experts(u_tab, h2t, vt_tab, route_out, x1, mod3, seq):
    n_e, d = u_tab.shape
    t = h2t.shape[1]
    tm = _pick(seq, 512)
    te = _pick(n_e, 512)
    ne = n_e // te
    spb = seq // tm
    kern = functools.partial(_experts_kernel, te=te, tm=tm, ne=ne)
    rspec = pl.BlockSpec((PEER_HEADS, PEER_N_KEYS, tm), lambda i, e: (0, 0, i))
    return pl.pallas_call(
        kern,
        grid=(t // tm, ne + 1),
        in_specs=[
            pl.BlockSpec((te, d), lambda i, e: (jnp.minimum(e, ne - 1), 0)),
            pl.BlockSpec((d, tm), lambda i, e: (0, i)),
            pl.BlockSpec((d, te), lambda i, e: (0, jnp.maximum(e - 1, 0))),
            rspec, rspec, rspec,
            pl.BlockSpec((tm, d), lambda i, e: (i, 0), pipeline_mode=pl.Buffered(1)),
            pl.BlockSpec((1, N_ADA, d), lambda i, e: (i // spb, 0, 0)),
        ],
        out_specs=pl.BlockSpec((tm, d), lambda i, e: (i, 0), pipeline_mode=pl.Buffered(1)),
        out_shape=jax.ShapeDtypeStruct((t, d), F32),
        scratch_shapes=[pltpu.VMEM((d, tm), F32), pltpu.VMEM((te, tm), F32), pltpu.VMEM((te, tm), F32)],
        compiler_params=_params(("parallel", "arbitrary")),
        name="experts",
    )(u_tab, h2t, vt_tab, *route_out, x1, mod3)


def kernel(x, c, w_ada, b_ada, norm1_g, w_in, lam_re, lam_im, log_dt, ssm_b_re, ssm_b_im, ssm_c_re, ssm_c_im, ssm_d, w_glu, q_norm_g, k_norm_g, w_att_up, w_out, norm2_g, peer_wq, peer_k1, peer_k2, peer_u, peer_v):
    bsz, seq, d = x.shape
    depth = w_ada.shape[0]
    ssm_width = ssm_d.shape[1] * ssm_d.shape[2]
    sb_width = SB_HEADS * SB_HEAD_DIM
    assert ssm_width == sb_width and d % sb_width == 0
    assert ssm_d.shape[1] % S5_CHUNKS == 0 and ssm_d.shape[2] == SSM_GROUP and lam_re.shape[2] == SSM_STATE
    q_col = sb_width // SB_HEAD_DIM
    k_col = 2 * q_col
    v_col = 3 * q_col
    gate_base = 4 * sb_width
    assert gate_base % d == 0
    gs_col = gate_base // d
    ga_col = gs_col + 1

    x2 = x.reshape(bsz * seq, d)
    c_pad = jnp.zeros((SUBLANES, d), F32).at[:bsz].set(c)
    for l in range(depth):
        mod = _ada(c_pad, w_ada[l], b_ada[l].reshape(1, -1))
        mod3 = mod.reshape(SUBLANES, N_ADA, d)
        proj = _inproj(x2, mod3, norm1_g[l].reshape(1, d), w_in[l].astype(BF16),
                       q_norm_g[l].reshape(1, -1), k_norm_g[l].reshape(1, -1), seq, sb_width)
        s5_ops = _s5_operands(lam_re[l], lam_im[l], log_dt[l], ssm_b_re[l], ssm_b_im[l],
                              ssm_c_re[l], ssm_c_im[l], ssm_d[l])
        ygelu = _s5(proj, s5_ops, bsz, seq, ssm_width)
        att = _attn(proj, bsz, seq, q_col, k_col, v_col)
        x1, h2, h2t = _mix(ygelu, att, proj, x2, mod3, w_glu[l].astype(BF16), w_att_up[l].astype(BF16),
                           w_out[l].astype(BF16), norm2_g[l].reshape(1, d), seq, gs_col, ga_col)
        route_out = _route(h2, peer_wq[l].astype(BF16), peer_k1[l].astype(BF16), peer_k2[l].astype(BF16))
        x2 = _experts(peer_u[l].astype(BF16), h2t, peer_v[l].T.astype(BF16), route_out, x1, mod3, seq)
    return x2.reshape(bsz, seq, d)
```

```python
import functools
import math

import jax
import jax.numpy as jnp
from jax import lax
from jax.experimental import pallas as pl
from jax.experimental.pallas import tpu as pltpu

F32 = jnp.float32
BF16 = jnp.bfloat16

LANES = 128
SUBLANES = 8
VMEM_LIMIT = 56 * 1024 * 1024

SSM_GROUP = 16
SSM_STATE = 64
SB_HEADS = 8
SB_HEAD_DIM = 128
PEER_HEADS = 8
PEER_N_KEYS = 128
PEER_TOPK = 16
PEER_HALF = 128
N_ADA = 6
EPS = 1e-6
NEG_INF = float("-inf")
LOG2E = math.log2(math.e)


def _params(sem):
    return pltpu.CompilerParams(dimension_semantics=sem, vmem_limit_bytes=VMEM_LIMIT)


GELU_C = -2.0 * math.sqrt(2.0 / math.pi) * LOG2E


def _gelu(x):
    return x / (1.0 + jnp.exp2(x * (GELU_C + (GELU_C * 0.044715) * (x * x))))


def _pick(n, pref):
    t = min(n, pref)
    while n % t:
        t //= 2
    return t


def _ada_kernel(c_ref, w_ref, b_ref, o_ref):
    c = c_ref[...]
    c = c * jax.nn.sigmoid(c)
    o_ref[...] = jnp.dot(c, w_ref[...], preferred_element_type=F32) + b_ref[...]


def _ada(c_pad, w, b):
    rows, d = c_pad.shape
    n = w.shape[1]
    tn = _pick(n, 1024)
    return pl.pallas_call(
        _ada_kernel,
        grid=(n // tn,),
        in_specs=[
            pl.BlockSpec((rows, d), lambda j: (0, 0)),
            pl.BlockSpec((d, tn), lambda j: (0, j)),
            pl.BlockSpec((1, tn), lambda j: (0, j)),
        ],
        out_specs=pl.BlockSpec((rows, tn), lambda j: (0, j)),
        out_shape=jax.ShapeDtypeStruct((rows, n), F32),
        compiler_params=_params(("arbitrary",)),
        name="ada",
    )(c_pad, w, b)


def _head_rms(acc, g_ref, o_ref, scale):
    for h in range(acc.shape[1] // SB_HEAD_DIM):
        blk = acc[:, h * SB_HEAD_DIM:(h + 1) * SB_HEAD_DIM]
        ms = jnp.mean(blk * blk, axis=-1, keepdims=True)
        y = blk * lax.rsqrt(ms + EPS) * g_ref[...]
        if scale != 1.0:
            y = y * scale
        o_ref[:, h * SB_HEAD_DIM:(h + 1) * SB_HEAD_DIM] = y.astype(o_ref.dtype)


def _inproj_kernel(x_ref, mod_ref, g_ref, w_ref, qg_ref, kg_ref, o_ref, h_scr, *, q_blk, k_blk):
    j = pl.program_id(1)

    @pl.when(j == 0)
    def _():
        x = x_ref[...]
        ms = jnp.mean(x * x, axis=-1, keepdims=True)
        y = x * lax.rsqrt(ms + EPS) * g_ref[...]
        h = y * (1.0 + mod_ref[0, 1:2, :]) + mod_ref[0, 0:1, :]
        h_scr[...] = h.astype(BF16)

    acc = jnp.dot(h_scr[...], w_ref[...], preferred_element_type=F32)

    @pl.when(j == q_blk)
    def _():
        _head_rms(acc, qg_ref, o_ref, SB_HEAD_DIM ** -0.5 * LOG2E)

    @pl.when(j == k_blk)
    def _():
        _head_rms(acc, kg_ref, o_ref, 1.0)

    @pl.when(jnp.logical_and(j != q_blk, j != k_blk))
    def _():
        o_ref[...] = acc.astype(o_ref.dtype)


def _inproj(x2, mod3, g, w, qg, kg, seq, sb_width):
    t, d = x2.shape
    n = w.shape[1]
    tm = _pick(seq, 1024)
    tn = sb_width
    steps_per_batch = seq // tm
    kern = functools.partial(_inproj_kernel, q_blk=1, k_blk=2)
    return pl.pallas_call(
        kern,
        grid=(t // tm, n // tn),
        in_specs=[
            pl.BlockSpec((tm, d), lambda i, j: (i, 0)),
            pl.BlockSpec((1, N_ADA, d), lambda i, j: (i // steps_per_batch, 0, 0)),
            pl.BlockSpec((1, d), lambda i, j: (0, 0)),
            pl.BlockSpec((d, tn), lambda i, j: (0, j)),
            pl.BlockSpec((1, SB_HEAD_DIM), lambda i, j: (0, 0)),
            pl.BlockSpec((1, SB_HEAD_DIM), lambda i, j: (0, 0)),
        ],
        out_specs=pl.BlockSpec((tm, tn), lambda i, j: (i, j)),
        out_shape=jax.ShapeDtypeStruct((t, n), BF16),
        scratch_shapes=[pltpu.VMEM((tm, d), BF16)],
        compiler_params=_params(("parallel", "arbitrary")),
        name="inproj",
    )(x2, mod3, g, w, qg, kg)


S5_CHUNKS = 8
S5_CHUNK_IN = LANES
S5_CHUNK_STATE = 8 * SSM_STATE
S5_SLABS = S5_CHUNK_STATE // LANES
S5_PITCH_PAD = SUBLANES


def _s5_kernel(u_ref, br_ref, bi_ref, cr_ref, ci_ref, lamr_ref, lami_ref, d_ref, o_ref,
               bur, bui, xrs, xis, sr, si, *, tm, pitch):
    @pl.when(pl.program_id(1) == 0)
    def _():
        sr[...] = jnp.zeros_like(sr)
        si[...] = jnp.zeros_like(si)

    for c in range(S5_CHUNKS):
        uc = u_ref[:, c * S5_CHUNK_IN:(c + 1) * S5_CHUNK_IN]
        r = jnp.dot(uc, br_ref[c], preferred_element_type=F32)
        im = jnp.dot(uc, bi_ref[c], preferred_element_type=F32)
        for l in range(S5_SLABS):
            bur[l, c * pitch:c * pitch + tm, :] = r[:, l * LANES:(l + 1) * LANES]
            bui[l, c * pitch:c * pitch + tm, :] = im[:, l * LANES:(l + 1) * LANES]

    lamr = [lamr_ref[:, l * LANES:(l + 1) * LANES] for l in range(S5_SLABS)]
    lami = [lami_ref[:, l * LANES:(l + 1) * LANES] for l in range(S5_SLABS)]

    def body(t, carry):
        xr, xi = carry
        nxr, nxi = [], []
        for l in range(S5_SLABS):
            rows = pl.ds(t, S5_CHUNKS, stride=pitch)
            b_r = bur[l, rows, :]
            b_i = bui[l, rows, :]
            n_r = lamr[l] * xr[l] - lami[l] * xi[l] + b_r
            n_i = lamr[l] * xi[l] + lami[l] * xr[l] + b_i
            xrs[l, rows, :] = n_r
            xis[l, rows, :] = n_i
            nxr.append(n_r)
            nxi.append(n_i)
        return tuple(nxr), tuple(nxi)

    init = (tuple(sr[:, l * LANES:(l + 1) * LANES] for l in range(S5_SLABS)),
            tuple(si[:, l * LANES:(l + 1) * LANES] for l in range(S5_SLABS)))
    xr, xi = lax.fori_loop(0, tm, body, init, unroll=8)
    for l in range(S5_SLABS):
        sr[:, l * LANES:(l + 1) * LANES] = xr[l]
        si[:, l * LANES:(l + 1) * LANES] = xi[l]

    for c in range(S5_CHUNKS):
        acc = jnp.zeros((tm, S5_CHUNK_IN), F32)
        for l in range(S5_SLABS):
            xr_c = xrs[l, c * pitch:c * pitch + tm, :].astype(BF16)
            xi_c = xis[l, c * pitch:c * pitch + tm, :].astype(BF16)
            acc = acc + jnp.dot(xr_c, cr_ref[c, l * LANES:(l + 1) * LANES, :], preferred_element_type=F32)
            acc = acc + jnp.dot(xi_c, ci_ref[c, l * LANES:(l + 1) * LANES, :], preferred_element_type=F32)
        cols = slice(c * S5_CHUNK_IN, (c + 1) * S5_CHUNK_IN)
        y = acc + d_ref[:, cols] * u_ref[:, cols].astype(F32)
        o_ref[:, cols] = _gelu(y).astype(o_ref.dtype)


def _s5_operands(lam_re, lam_im, log_dt, b_re, b_im, c_re, c_im, d_skip):
    g, p = lam_re.shape
    i_dim = b_re.shape[-1]
    dt = jnp.exp(log_dt)[:, None]
    mag = jnp.exp(lam_re * dt)
    lbr = mag * jnp.cos(lam_im * dt)
    lbi = mag * jnp.sin(lam_im * dt)
    nr, ni = lbr - 1.0, lbi
    den = lam_re * lam_re + lam_im * lam_im
    cfr = (nr * lam_re + ni * lam_im) / den
    cfi = (ni * lam_re - nr * lam_im) / den
    bbr = cfr[..., None] * b_re - cfi[..., None] * b_im
    bbi = cfr[..., None] * b_im + cfi[..., None] * b_re
    gpc = g // S5_CHUNKS
    eye = jnp.eye(gpc, dtype=F32)

    def pack_b(m):
        m = m.reshape(S5_CHUNKS, gpc, p, i_dim)
        return jnp.einsum("cgpi,gh->cgihp", m, eye).reshape(S5_CHUNKS, gpc * i_dim, gpc * p)

    def pack_c(m):
        m = m.reshape(S5_CHUNKS, gpc, i_dim, p)
        return jnp.einsum("cgip,gh->cgphi", m, eye).reshape(S5_CHUNKS, gpc * p, gpc * i_dim)

    return (pack_b(bbr).astype(BF16), pack_b(bbi).astype(BF16),
            pack_c(c_re).astype(BF16), pack_c(-c_im).astype(BF16),
            lbr.reshape(S5_CHUNKS, gpc * p), lbi.reshape(S5_CHUNKS, gpc * p),
            d_skip.reshape(1, g * i_dim))


def _s5(proj, ops, bsz, seq, width):
    br, bi, cr, ci, lamr, lami, dsk = ops
    tm = _pick(seq, 256)
    pitch = tm + S5_PITCH_PAD
    nt = seq // tm
    kern = functools.partial(_s5_kernel, tm=tm, pitch=pitch)
    full = lambda a: pl.BlockSpec(a.shape, lambda b, i: (0,) * a.ndim)
    scr = pltpu.VMEM((S5_SLABS, S5_CHUNKS * pitch, LANES), F32)
    return pl.pallas_call(
        kern,
        grid=(bsz, nt),
        in_specs=[pl.BlockSpec((tm, width), lambda b, i: (b * nt + i, 0)),
                  full(br), full(bi), full(cr), full(ci), full(lamr), full(lami), full(dsk)],
        out_specs=pl.BlockSpec((tm, width), lambda b, i: (b * nt + i, 0)),
        out_shape=jax.ShapeDtypeStruct((bsz * seq, width), BF16),
        scratch_shapes=[scr, scr, scr, scr,
                        pltpu.VMEM((S5_CHUNKS, S5_CHUNK_STATE), F32),
                        pltpu.VMEM((S5_CHUNKS, S5_CHUNK_STATE), F32)],
        compiler_params=_params(("parallel", "arbitrary")),
        name="s5",
    )(proj, br, bi, cr, ci, lamr, lami, dsk)


ATTN_GROUP = 4
ATTN_HEADS = 2


def _neg_abs(z):
    bits = lax.bitcast_convert_type(z, jnp.uint32) | jnp.uint32(0x80000000)
    return lax.bitcast_convert_type(bits, F32)


def _attn_kernel(q_ref, k_ref, v_ref, tri_ref, o_ref, acc_ref, carry_ref, *, tb):
    qi = pl.program_id(2)
    acc_ref[...] = jnp.zeros_like(acc_ref)
    carry_ref[...] = jnp.zeros_like(carry_ref)

    def group(kb_lo, n, diagonal):
        ks = pl.multiple_of(kb_lo * tb, tb)
        for hh in range(ATTN_HEADS):
            cols = slice(hh * SB_HEAD_DIM, (hh + 1) * SB_HEAD_DIM)
            q = q_ref[:, cols]
            k = k_ref[pl.ds(ks, n * tb), cols]
            v = v_ref[pl.ds(ks, n * tb), cols]
            carry = carry_ref[hh]
            ws = [None] * n
            for g in reversed(range(n)):
                z = lax.dot_general(q, k[g * tb:(g + 1) * tb, :], (((1,), (1,)), ((), ())),
                                    preferred_element_type=F32)
                sp = jnp.maximum(z, 0.0) + jnp.log(1.0 + jnp.exp2(_neg_abs(z))) * LOG2E
                masked = diagonal and g == n - 1
                if masked:
                    row = lax.broadcasted_iota(jnp.int32, (tb, tb), 0)
                    col = lax.broadcasted_iota(jnp.int32, (tb, tb), 1)
                    causal = col < row
                    sp_sum = jnp.where(causal, sp, 0.0)
                else:
                    sp_sum = sp
                cum = jnp.dot(sp_sum.astype(BF16), tri_ref[...], preferred_element_type=F32)
                w = jnp.exp2(z - sp - cum - carry)
                if masked:
                    w = jnp.where(causal, w, 0.0)
                ws[g] = w.astype(BF16)
                carry = carry + jnp.sum(sp_sum, axis=1, keepdims=True)
            w_all = ws[0] if n == 1 else jnp.concatenate(ws, axis=1)
            acc_ref[:, cols] += jnp.dot(w_all, v, preferred_element_type=F32)
            carry_ref[hh] = carry

    group(qi, 1, True)
    done = jnp.int32(0)
    size = 1
    while size < ATTN_GROUP:
        take = (qi & size) != 0

        @pl.when(take)
        def _(done=done, size=size):
            group(qi - done - size, size, False)

        done = done + jnp.where(take, size, 0)
        size *= 2

    def multi(it, _):
        group(qi - done - (it + 1) * ATTN_GROUP, ATTN_GROUP, False)
        return 0

    lax.fori_loop(0, qi // ATTN_GROUP, multi, 0)
    o_ref[...] = acc_ref[...].astype(o_ref.dtype)


def _attn(proj, bsz, seq, q_col, k_col, v_col):
    tb = _pick(seq, 256)
    nq = seq // tb
    r = lax.broadcasted_iota(jnp.int32, (tb, tb), 0)
    c = lax.broadcasted_iota(jnp.int32, (tb, tb), 1)
    tri = jnp.where(r > c, 1.0, 0.0).astype(BF16)
    kern = functools.partial(_attn_kernel, tb=tb)
    hw = ATTN_HEADS * SB_HEAD_DIM
    assert q_col % ATTN_HEADS == 0 and SB_HEADS % ATTN_HEADS == 0
    qc, kc, vc = q_col // ATTN_HEADS, k_col // ATTN_HEADS, v_col // ATTN_HEADS
    return pl.pallas_call(
        kern,
        grid=(bsz, SB_HEADS // ATTN_HEADS, nq),
        in_specs=[
            pl.BlockSpec((tb, hw), lambda b, h, i: (b * nq + i, qc + h)),
            pl.BlockSpec((seq, hw), lambda b, h, i: (b, kc + h)),
            pl.BlockSpec((seq, hw), lambda b, h, i: (b, vc + h)),
            pl.BlockSpec((tb, tb), lambda b, h, i: (0, 0)),
        ],
        out_specs=pl.BlockSpec((tb, hw), lambda b, h, i: (b * nq + i, h)),
        out_shape=jax.ShapeDtypeStruct((bsz * seq, SB_HEADS * SB_HEAD_DIM), BF16),
        scratch_shapes=[pltpu.VMEM((tb, hw), F32), pltpu.VMEM((ATTN_HEADS, tb, 1), F32)],
        compiler_params=_params(("parallel", "parallel", "arbitrary")),
        name="attn",
    )(proj, proj, proj, tri)


def _mix_kernel(y_ref, att_ref, gs_ref, ga_ref, x_ref, mod_ref, wglu_ref, wup_ref, wout_ref, g2_ref,
                x1_ref, h2_ref, h2t_ref, *, d):
    yg = jnp.dot(y_ref[...], wglu_ref[...], preferred_element_type=F32)
    ssm = yg[:, :d] * jax.nn.sigmoid(yg[:, d:])
    up = jnp.dot(att_ref[...], wup_ref[...], preferred_element_type=F32)
    merged = (jax.nn.sigmoid(gs_ref[...].astype(F32)) * ssm
              + jax.nn.sigmoid(ga_ref[...].astype(F32)) * up)
    o = jnp.dot(merged.astype(BF16), wout_ref[...], preferred_element_type=F32)
    x1 = x_ref[...] + mod_ref[0, 2:3, :] * o
    x1_ref[...] = x1
    ms = jnp.mean(x1 * x1, axis=-1, keepdims=True)
    h2 = x1 * lax.rsqrt(ms + EPS) * g2_ref[...]
    h2 = h2 * (1.0 + mod_ref[0, 4:5, :]) + mod_ref[0, 3:4, :]
    h2_ref[...] = h2.astype(BF16)
    h2t_ref[...] = h2.T.astype(BF16)


def _mix(ygelu, att, proj, x2, mod3, wglu, wup, wout, g2, seq, gs_col, ga_col):
    t, d = x2.shape
    w = ygelu.shape[1]
    tm = _pick(seq, 256)
    spb = seq // tm
    kern = functools.partial(_mix_kernel, d=d)
    const = lambda a: pl.BlockSpec(a.shape, lambda i: (0,) * a.ndim)
    return pl.pallas_call(
        kern,
        grid=(t // tm,),
        in_specs=[
            pl.BlockSpec((tm, w), lambda i: (i, 0)),
            pl.BlockSpec((tm, w), lambda i: (i, 0)),
            pl.BlockSpec((tm, d), lambda i: (i, gs_col)),
            pl.BlockSpec((tm, d), lambda i: (i, ga_col)),
            pl.BlockSpec((tm, d), lambda i: (i, 0)),
            pl.BlockSpec((1, N_ADA, d), lambda i: (i // spb, 0, 0)),
            const(wglu), const(wup), const(wout), const(g2),
        ],
        out_specs=[
            pl.BlockSpec((tm, d), lambda i: (i, 0)),
            pl.BlockSpec((tm, d), lambda i: (i, 0)),
            pl.BlockSpec((d, tm), lambda i: (0, i)),
        ],
        out_shape=[jax.ShapeDtypeStruct((t, d), F32),
                   jax.ShapeDtypeStruct((t, d), BF16),
                   jax.ShapeDtypeStruct((d, t), BF16)],
        compiler_params=_params(("parallel",)),
        name="mix",
    )(ygelu, att, proj, proj, x2, mod3, wglu, wup, wout, g2)


PEER_RANKS = PEER_TOPK + 1
PEER_RANK_ROWS = 24


def _candidate_tiles(v1_scr, v2_scr, tm):
    tiles = []
    row8 = lax.broadcasted_iota(jnp.int32, (SUBLANES, tm), 0)
    for b in range(PEER_RANKS):
        n_a = PEER_RANKS // (b + 1)
        v2b = v2_scr[b:b + 1, :]
        for base in range(0, n_a, SUBLANES):
            t = v1_scr[base:base + SUBLANES, :] + v2b
            if n_a - base < SUBLANES:
                t = jnp.where(row8 < (n_a - base), t, NEG_INF)
            tiles.append(t)
    return tiles


def _sort_network(n):
    pairs = []
    p = 1
    while p < n:
        k = p
        while k >= 1:
            for j in range(k % p, n - k, 2 * k):
                for i in range(min(k, n - j - k)):
                    if (i + j) // (2 * p) == (i + j + k) // (2 * p):
                        pairs.append((i + j, i + j + k))
            k //= 2
        p *= 2
    return pairs


def _top_values(s, v_scr):
    n_tiles = s.shape[0] // SUBLANES
    rows = [s[k * SUBLANES:(k + 1) * SUBLANES, :] for k in range(n_tiles)]
    for i, j in _sort_network(n_tiles):
        rows[i], rows[j] = jnp.maximum(rows[i], rows[j]), jnp.minimum(rows[i], rows[j])
    for a in range(PEER_RANKS):
        m = jnp.max(rows[0], axis=0, keepdims=True)
        v_scr[a:a + 1, :] = m
        hit = rows[0] >= m
        for k in range(min(n_tiles, PEER_RANKS - 1 - a)):
            below = rows[k + 1] if k + 1 < n_tiles else NEG_INF
            rows[k] = jnp.where(hit, below, rows[k])


def _route_kernel(h2_ref, wq_ref, k1_ref, k2_ref, c1_ref, w1_ref, e2_ref,
                  s_scr, v1_scr, v2_scr, *, tm):
    q = jnp.dot(h2_ref[...], wq_ref[...], preferred_element_type=F32).astype(k1_ref.dtype)
    v1_scr[...] = jnp.full(v1_scr.shape, NEG_INF, F32)
    v2_scr[...] = jnp.full(v2_scr.shape, NEG_INF, F32)
    for h in range(PEER_HEADS):
        for half, (k_ref, v_scr) in enumerate(((k1_ref, v1_scr), (k2_ref, v2_scr))):
            col = (2 * h + half) * PEER_HALF
            qh = q[:, col:col + PEER_HALF]
            s = lax.dot_general(k_ref[...], qh, (((1,), (1,)), ((), ())), preferred_element_type=F32)
            s_scr[half] = s
            _top_values(s, v_scr)
        tiles = _candidate_tiles(v1_scr, v2_scr, tm)
        work = list(tiles)
        ranked = []
        for a in range(PEER_RANKS):
            m = work[0]
            for t in work[1:]:
                m = jnp.maximum(m, t)
            m = jnp.max(m, axis=0, keepdims=True)
            ranked.append(m)
            work = [jnp.where(t >= m, NEG_INF, t) for t in work]
        top = ranked[0]
        thr = 0.5 * (ranked[PEER_TOPK - 1] + ranked[PEER_TOPK])
        zsum = jnp.zeros((SUBLANES, tm), F32)
        for t in tiles:
            zsum = zsum + jnp.where(t >= thr, jnp.exp(t - top), 0.0)
        inv_z = 1.0 / jnp.sum(zsum, axis=0, keepdims=True)
        s1 = s_scr[0]
        s2 = s_scr[1]
        top2 = v2_scr[0:1, :]
        c1_ref[h] = jnp.exp(thr - s1 - top2)
        w1_ref[h] = jnp.exp(s1 - v1_scr[0:1, :]) * inv_z
        e2_ref[h] = jnp.exp(s2 - top2)


def _route(h2, wq, k1, k2):
    t, d = h2.shape
    tm = _pick(t, 256)
    kern = functools.partial(_route_kernel, tm=tm)
    const = lambda a: pl.BlockSpec(a.shape, lambda i: (0,) * a.ndim)
    out_spec = pl.BlockSpec((PEER_HEADS, PEER_N_KEYS, tm), lambda i: (0, 0, i))
    out_sds = jax.ShapeDtypeStruct((PEER_HEADS, PEER_N_KEYS, t), F32)
    return pl.pallas_call(
        kern,
        grid=(t // tm,),
        in_specs=[pl.BlockSpec((tm, d), lambda i: (i, 0)), const(wq), const(k1), const(k2)],
        out_specs=[out_spec] * 3,
        out_shape=[out_sds] * 3,
        scratch_shapes=[pltpu.VMEM((2, PEER_N_KEYS, tm), F32),
                        pltpu.VMEM((PEER_RANK_ROWS, tm), F32),
                        pltpu.VMEM((PEER_RANK_ROWS, tm), F32)],
        compiler_params=_params(("parallel",)),
        name="route",
    )(h2, wq, k1, k2)


EXPERT_TILE = 1024


def _expert_gated(i_first, n_i, row0, act_r, c1_ref, w1_ref, e2_ref, tm):
    ps = []
    for ii in range(n_i):
        i = i_first + ii
        rows = slice(row0 + ii * PEER_N_KEYS, row0 + (ii + 1) * PEER_N_KEYS)
        c_rows = [c1_ref[h, pl.ds(i, 1), :] for h in range(PEER_HEADS)]
        w_rows = [w1_ref[h, pl.ds(i, 1), :] for h in range(PEER_HEADS)]
        chunks = []
        for tc in range(tm // LANES):
            cols = slice(tc * LANES, (tc + 1) * LANES)
            gate = None
            for h in range(PEER_HEADS):
                e2 = e2_ref[h, :, cols]
                term = jnp.where(e2 >= c_rows[h][:, cols], e2 * w_rows[h][:, cols], 0.0)
                gate = term if gate is None else gate + term
            chunks.append((gate * _gelu(act_r[rows, cols])).astype(BF16))
        ps.append(jnp.concatenate(chunks, axis=1))
    return ps[0] if len(ps) == 1 else jnp.concatenate(ps, axis=0)


def _experts_kernel(u_ref, h2t_ref, vt_ref, c1_ref, w1_ref, e2_ref, x1_ref, mod_ref, o_ref,
                    acc_ref, act0, act1, *, te, tm, ne):
    e = pl.program_id(1)
    n_i = te // PEER_N_KEYS
    slots = (act0, act1)

    def step(act_w, act_r, i0):
        if act_w is not None:
            act_w[...] = jnp.dot(u_ref[...], h2t_ref[...], preferred_element_type=F32)
        if act_r is not None:
            p = _expert_gated(i0, n_i, 0, act_r, c1_ref, w1_ref, e2_ref, tm)
            acc_ref[...] += jnp.dot(vt_ref[0], p, preferred_element_type=F32)

    @pl.when(e == 0)
    def _():
        acc_ref[...] = jnp.zeros_like(acc_ref)
        step(slots[0], None, None)

    for parity in range(2):
        @pl.when(jnp.logical_and(jnp.logical_and(e > 0, e < ne), lax.rem(e, 2) == parity))
        def _(parity=parity):
            step(slots[parity], slots[1 - parity], (e - 1) * n_i)

    @pl.when(e == ne)
    def _():
        step(None, slots[(ne - 1) % 2], (ne - 1) * n_i)
        o_ref[...] = x1_ref[...] + mod_ref[0, 5:6, :] * acc_ref[...].T


def _experts(u_tab, h2t, v_tab, route_out, x1, mod3, seq):
    n_e, d = u_tab.shape
    t = h2t.shape[1]
    tm = _pick(seq, 512)
    te = _pick(n_e, EXPERT_TILE)
    ne = n_e // te
    spb = seq // tm
    vt_tiles = v_tab.reshape(ne, te, d).transpose(0, 2, 1).astype(BF16)
    kern = functools.partial(_experts_kernel, te=te, tm=tm, ne=ne)
    rspec = pl.BlockSpec((PEER_HEADS, PEER_N_KEYS, tm), lambda i, e: (0, 0, i))
    return pl.pallas_call(
        kern,
        grid=(t // tm, ne + 1),
        in_specs=[
            pl.BlockSpec((te, d), lambda i, e: (jnp.minimum(e, ne - 1), 0)),
            pl.BlockSpec((d, tm), lambda i, e: (0, i)),
            pl.BlockSpec((1, d, te), lambda i, e: (jnp.maximum(e - 1, 0), 0, 0)),
            rspec, rspec, rspec,
            pl.BlockSpec((tm, d), lambda i, e: (i, 0), pipeline_mode=pl.Buffered(1)),
            pl.BlockSpec((1, N_ADA, d), lambda i, e: (i // spb, 0, 0)),
        ],
        out_specs=pl.BlockSpec((tm, d), lambda i, e: (i, 0), pipeline_mode=pl.Buffered(1)),
        out_shape=jax.ShapeDtypeStruct((t, d), F32),
        scratch_shapes=[pltpu.VMEM((d, tm), F32), pltpu.VMEM((te, tm), F32), pltpu.VMEM((te, tm), F32)],
        compiler_params=_params(("parallel", "arbitrary")),
        name="experts",
    )(u_tab, h2t, vt_tiles, *route_out, x1, mod3)


def kernel(x, c, w_ada, b_ada, norm1_g, w_in, lam_re, lam_im, log_dt, ssm_b_re, ssm_b_im, ssm_c_re, ssm_c_im, ssm_d, w_glu, q_norm_g, k_norm_g, w_att_up, w_out, norm2_g, peer_wq, peer_k1, peer_k2, peer_u, peer_v):
    bsz, seq, d = x.shape
    depth = w_ada.shape[0]
    ssm_width = ssm_d.shape[1] * ssm_d.shape[2]
    sb_width = SB_HEADS * SB_HEAD_DIM
    assert ssm_width == sb_width and d % sb_width == 0
    assert ssm_d.shape[1] % S5_CHUNKS == 0 and ssm_d.shape[2] == SSM_GROUP and lam_re.shape[2] == SSM_STATE
    q_col = sb_width // SB_HEAD_DIM
    k_col = 2 * q_col
    v_col = 3 * q_col
    gate_base = 4 * sb_width
    assert gate_base % d == 0
    gs_col = gate_base // d
    ga_col = gs_col + 1

    x2 = x.reshape(bsz * seq, d)
    c_pad = jnp.zeros((SUBLANES, d), F32).at[:bsz].set(c)
    for l in range(depth):
        mod = _ada(c_pad, w_ada[l], b_ada[l].reshape(1, -1))
        mod3 = mod.reshape(SUBLANES, N_ADA, d)
        proj = _inproj(x2, mod3, norm1_g[l].reshape(1, d), w_in[l].astype(BF16),
                       q_norm_g[l].reshape(1, -1), k_norm_g[l].reshape(1, -1), seq, sb_width)
        s5_ops = _s5_operands(lam_re[l], lam_im[l], log_dt[l], ssm_b_re[l], ssm_b_im[l],
                              ssm_c_re[l], ssm_c_im[l], ssm_d[l])
        ygelu = _s5(proj, s5_ops, bsz, seq, ssm_width)
        att = _attn(proj, bsz, seq, q_col, k_col, v_col)
        x1, h2, h2t = _mix(ygelu, att, proj, x2, mod3, w_glu[l].astype(BF16), w_att_up[l].astype(BF16),
                           w_out[l].astype(BF16), norm2_g[l].reshape(1, d), seq, gs_col, ga_col)
        route_out = _route(h2, peer_wq[l].astype(BF16), peer_k1[l].astype(BF16), peer_k2[l].astype(BF16))
        x2 = _experts(peer_u[l].astype(BF16), h2t, peer_v[l], route_out, x1, mod3, seq)
    return x2.reshape(bsz, seq, d)
```

```python
import functools
import math

import jax
import jax.numpy as jnp
from jax import lax
from jax.experimental import pallas as pl
from jax.experimental.pallas import tpu as pltpu

F32 = jnp.float32
BF16 = jnp.bfloat16

LANES = 128
SUBLANES = 8
VMEM_LIMIT = 56 * 1024 * 1024

SSM_GROUP = 16
SSM_STATE = 64
SB_HEADS = 8
SB_HEAD_DIM = 128
PEER_HEADS = 8
PEER_N_KEYS = 128
PEER_TOPK = 16
PEER_HALF = 128
N_ADA = 6
EPS = 1e-6
NEG_INF = float("-inf")
LOG2E = math.log2(math.e)


def _params(sem):
    return pltpu.CompilerParams(dimension_semantics=sem, vmem_limit_bytes=VMEM_LIMIT)


GELU_C = -2.0 * math.sqrt(2.0 / math.pi) * LOG2E


def _gelu(x):
    return x / (1.0 + jnp.exp2(x * (GELU_C + (GELU_C * 0.044715) * (x * x))))


def _pick(n, pref):
    t = min(n, pref)
    while n % t:
        t //= 2
    return t


def _ada_kernel(c_ref, w_ref, b_ref, o_ref):
    c = c_ref[...]
    c = c * jax.nn.sigmoid(c)
    o_ref[...] = jnp.dot(c, w_ref[...], preferred_element_type=F32) + b_ref[...]


def _ada(c_pad, w, b):
    rows, d = c_pad.shape
    n = w.shape[1]
    tn = _pick(n, 1024)
    return pl.pallas_call(
        _ada_kernel,
        grid=(n // tn,),
        in_specs=[
            pl.BlockSpec((rows, d), lambda j: (0, 0)),
            pl.BlockSpec((d, tn), lambda j: (0, j)),
            pl.BlockSpec((1, tn), lambda j: (0, j)),
        ],
        out_specs=pl.BlockSpec((rows, tn), lambda j: (0, j)),
        out_shape=jax.ShapeDtypeStruct((rows, n), F32),
        compiler_params=_params(("arbitrary",)),
        name="ada",
    )(c_pad, w, b)


def _head_rms(acc, g_ref, o_ref, scale):
    for h in range(acc.shape[1] // SB_HEAD_DIM):
        blk = acc[:, h * SB_HEAD_DIM:(h + 1) * SB_HEAD_DIM]
        ms = jnp.mean(blk * blk, axis=-1, keepdims=True)
        y = blk * lax.rsqrt(ms + EPS) * g_ref[...]
        if scale != 1.0:
            y = y * scale
        o_ref[:, h * SB_HEAD_DIM:(h + 1) * SB_HEAD_DIM] = y.astype(o_ref.dtype)


def _inproj_kernel(x_ref, mod_ref, g_ref, w_ref, qg_ref, kg_ref, o_ref, h_scr, *, q_blk, k_blk):
    j = pl.program_id(1)

    @pl.when(j == 0)
    def _():
        x = x_ref[...]
        ms = jnp.mean(x * x, axis=-1, keepdims=True)
        y = x * lax.rsqrt(ms + EPS) * g_ref[...]
        h = y * (1.0 + mod_ref[0, 1:2, :]) + mod_ref[0, 0:1, :]
        h_scr[...] = h.astype(BF16)

    acc = jnp.dot(h_scr[...], w_ref[...], preferred_element_type=F32)

    @pl.when(j == q_blk)
    def _():
        _head_rms(acc, qg_ref, o_ref, SB_HEAD_DIM ** -0.5 * LOG2E)

    @pl.when(j == k_blk)
    def _():
        _head_rms(acc, kg_ref, o_ref, 1.0)

    @pl.when(jnp.logical_and(j != q_blk, j != k_blk))
    def _():
        o_ref[...] = acc.astype(o_ref.dtype)


def _inproj(x2, mod3, g, w, qg, kg, seq, sb_width):
    t, d = x2.shape
    n = w.shape[1]
    tm = _pick(seq, 1024)
    tn = sb_width
    steps_per_batch = seq // tm
    kern = functools.partial(_inproj_kernel, q_blk=1, k_blk=2)
    return pl.pallas_call(
        kern,
        grid=(t // tm, n // tn),
        in_specs=[
            pl.BlockSpec((tm, d), lambda i, j: (i, 0)),
            pl.BlockSpec((1, N_ADA, d), lambda i, j: (i // steps_per_batch, 0, 0)),
            pl.BlockSpec((1, d), lambda i, j: (0, 0)),
            pl.BlockSpec((d, tn), lambda i, j: (0, j)),
            pl.BlockSpec((1, SB_HEAD_DIM), lambda i, j: (0, 0)),
            pl.BlockSpec((1, SB_HEAD_DIM), lambda i, j: (0, 0)),
        ],
        out_specs=pl.BlockSpec((tm, tn), lambda i, j: (i, j)),
        out_shape=jax.ShapeDtypeStruct((t, n), BF16),
        scratch_shapes=[pltpu.VMEM((tm, d), BF16)],
        compiler_params=_params(("parallel", "arbitrary")),
        name="inproj",
    )(x2, mod3, g, w, qg, kg)


S5_CHUNKS = 8
S5_CHUNK_IN = LANES
S5_CHUNK_STATE = 8 * SSM_STATE
S5_SLABS = S5_CHUNK_STATE // LANES
S5_PITCH_PAD = SUBLANES


def _s5_kernel(u_ref, br_ref, bi_ref, cr_ref, ci_ref, lamr_ref, lami_ref, d_ref, o_ref,
               bur, bui, xrs, xis, sr, si, *, tm, pitch):
    @pl.when(pl.program_id(1) == 0)
    def _():
        sr[...] = jnp.zeros_like(sr)
        si[...] = jnp.zeros_like(si)

    for c in range(S5_CHUNKS):
        uc = u_ref[:, c * S5_CHUNK_IN:(c + 1) * S5_CHUNK_IN]
        r = jnp.dot(uc, br_ref[c], preferred_element_type=F32)
        im = jnp.dot(uc, bi_ref[c], preferred_element_type=F32)
        for l in range(S5_SLABS):
            bur[l, c * pitch:c * pitch + tm, :] = r[:, l * LANES:(l + 1) * LANES]
            bui[l, c * pitch:c * pitch + tm, :] = im[:, l * LANES:(l + 1) * LANES]

    lamr = [lamr_ref[:, l * LANES:(l + 1) * LANES] for l in range(S5_SLABS)]
    lami = [lami_ref[:, l * LANES:(l + 1) * LANES] for l in range(S5_SLABS)]

    def body(t, carry):
        xr, xi = carry
        nxr, nxi = [], []
        for l in range(S5_SLABS):
            rows = pl.ds(t, S5_CHUNKS, stride=pitch)
            b_r = bur[l, rows, :]
            b_i = bui[l, rows, :]
            n_r = lamr[l] * xr[l] - lami[l] * xi[l] + b_r
            n_i = lamr[l] * xi[l] + lami[l] * xr[l] + b_i
            xrs[l, rows, :] = n_r
            xis[l, rows, :] = n_i
            nxr.append(n_r)
            nxi.append(n_i)
        return tuple(nxr), tuple(nxi)

    init = (tuple(sr[:, l * LANES:(l + 1) * LANES] for l in range(S5_SLABS)),
            tuple(si[:, l * LANES:(l + 1) * LANES] for l in range(S5_SLABS)))
    xr, xi = lax.fori_loop(0, tm, body, init, unroll=8)
    for l in range(S5_SLABS):
        sr[:, l * LANES:(l + 1) * LANES] = xr[l]
        si[:, l * LANES:(l + 1) * LANES] = xi[l]

    for c in range(S5_CHUNKS):
        acc = jnp.zeros((tm, S5_CHUNK_IN), F32)
        for l in range(S5_SLABS):
            xr_c = xrs[l, c * pitch:c * pitch + tm, :].astype(BF16)
            xi_c = xis[l, c * pitch:c * pitch + tm, :].astype(BF16)
            acc = acc + jnp.dot(xr_c, cr_ref[c, l * LANES:(l + 1) * LANES, :], preferred_element_type=F32)
            acc = acc + jnp.dot(xi_c, ci_ref[c, l * LANES:(l + 1) * LANES, :], preferred_element_type=F32)
        cols = slice(c * S5_CHUNK_IN, (c + 1) * S5_CHUNK_IN)
        y = acc + d_ref[:, cols] * u_ref[:, cols].astype(F32)
        o_ref[:, cols] = _gelu(y).astype(o_ref.dtype)


def _s5_operands(lam_re, lam_im, log_dt, b_re, b_im, c_re, c_im, d_skip):
    g, p = lam_re.shape
    i_dim = b_re.shape[-1]
    dt = jnp.exp(log_dt)[:, None]
    mag = jnp.exp(lam_re * dt)
    lbr = mag * jnp.cos(lam_im * dt)
    lbi = mag * jnp.sin(lam_im * dt)
    nr, ni = lbr - 1.0, lbi
    den = lam_re * lam_re + lam_im * lam_im
    cfr = (nr * lam_re + ni * lam_im) / den
    cfi = (ni * lam_re - nr * lam_im) / den
    bbr = cfr[..., None] * b_re - cfi[..., None] * b_im
    bbi = cfr[..., None] * b_im + cfi[..., None] * b_re
    gpc = g // S5_CHUNKS
    eye = jnp.eye(gpc, dtype=F32)

    def pack_b(m):
        m = m.reshape(S5_CHUNKS, gpc, p, i_dim)
        return jnp.einsum("cgpi,gh->cgihp", m, eye).reshape(S5_CHUNKS, gpc * i_dim, gpc * p)

    def pack_c(m):
        m = m.reshape(S5_CHUNKS, gpc, i_dim, p)
        return jnp.einsum("cgip,gh->cgphi", m, eye).reshape(S5_CHUNKS, gpc * p, gpc * i_dim)

    return (pack_b(bbr).astype(BF16), pack_b(bbi).astype(BF16),
            pack_c(c_re).astype(BF16), pack_c(-c_im).astype(BF16),
            lbr.reshape(S5_CHUNKS, gpc * p), lbi.reshape(S5_CHUNKS, gpc * p),
            d_skip.reshape(1, g * i_dim))


def _s5(proj, ops, bsz, seq, width):
    br, bi, cr, ci, lamr, lami, dsk = ops
    tm = _pick(seq, 256)
    pitch = tm + S5_PITCH_PAD
    nt = seq // tm
    kern = functools.partial(_s5_kernel, tm=tm, pitch=pitch)
    full = lambda a: pl.BlockSpec(a.shape, lambda b, i: (0,) * a.ndim)
    scr = pltpu.VMEM((S5_SLABS, S5_CHUNKS * pitch, LANES), F32)
    return pl.pallas_call(
        kern,
        grid=(bsz, nt),
        in_specs=[pl.BlockSpec((tm, width), lambda b, i: (b * nt + i, 0)),
                  full(br), full(bi), full(cr), full(ci), full(lamr), full(lami), full(dsk)],
        out_specs=pl.BlockSpec((tm, width), lambda b, i: (b * nt + i, 0)),
        out_shape=jax.ShapeDtypeStruct((bsz * seq, width), BF16),
        scratch_shapes=[scr, scr, scr, scr,
                        pltpu.VMEM((S5_CHUNKS, S5_CHUNK_STATE), F32),
                        pltpu.VMEM((S5_CHUNKS, S5_CHUNK_STATE), F32)],
        compiler_params=_params(("parallel", "arbitrary")),
        name="s5",
    )(proj, br, bi, cr, ci, lamr, lami, dsk)


ATTN_GROUP = 4
ATTN_HEADS = 2


def _neg_abs(z):
    bits = lax.bitcast_convert_type(z, jnp.uint32) | jnp.uint32(0x80000000)
    return lax.bitcast_convert_type(bits, F32)


def _attn_kernel(q_ref, k_ref, v_ref, tri_ref, o_ref, acc_ref, carry_ref, *, tb):
    qi = pl.program_id(2)
    acc_ref[...] = jnp.zeros_like(acc_ref)
    carry_ref[...] = jnp.zeros_like(carry_ref)

    def group(kb_lo, n, diagonal):
        ks = pl.multiple_of(kb_lo * tb, tb)
        for hh in range(ATTN_HEADS):
            cols = slice(hh * SB_HEAD_DIM, (hh + 1) * SB_HEAD_DIM)
            q = q_ref[:, cols]
            k = k_ref[pl.ds(ks, n * tb), cols]
            v = v_ref[pl.ds(ks, n * tb), cols]
            carry = carry_ref[hh]
            ws = [None] * n
            for g in reversed(range(n)):
                z = lax.dot_general(q, k[g * tb:(g + 1) * tb, :], (((1,), (1,)), ((), ())),
                                    preferred_element_type=F32)
                sp = jnp.maximum(z, 0.0) + jnp.log(1.0 + jnp.exp2(_neg_abs(z))) * LOG2E
                masked = diagonal and g == n - 1
                if masked:
                    row = lax.broadcasted_iota(jnp.int32, (tb, tb), 0)
                    col = lax.broadcasted_iota(jnp.int32, (tb, tb), 1)
                    causal = col < row
                    sp_sum = jnp.where(causal, sp, 0.0)
                else:
                    sp_sum = sp
                cum = jnp.dot(sp_sum.astype(BF16), tri_ref[...], preferred_element_type=F32)
                w = jnp.exp2(z - sp - cum - carry)
                if masked:
                    w = jnp.where(causal, w, 0.0)
                ws[g] = w.astype(BF16)
                carry = carry + jnp.sum(sp_sum, axis=1, keepdims=True)
            w_all = ws[0] if n == 1 else jnp.concatenate(ws, axis=1)
            acc_ref[:, cols] += jnp.dot(w_all, v, preferred_element_type=F32)
            carry_ref[hh] = carry

    group(qi, 1, True)
    done = jnp.int32(0)
    size = 1
    while size < ATTN_GROUP:
        take = (qi & size) != 0

        @pl.when(take)
        def _(done=done, size=size):
            group(qi - done - size, size, False)

        done = done + jnp.where(take, size, 0)
        size *= 2

    def multi(it, _):
        group(qi - done - (it + 1) * ATTN_GROUP, ATTN_GROUP, False)
        return 0

    lax.fori_loop(0, qi // ATTN_GROUP, multi, 0)
    o_ref[...] = acc_ref[...].astype(o_ref.dtype)


def _attn(proj, bsz, seq, q_col, k_col, v_col):
    tb = _pick(seq, 256)
    nq = seq // tb
    r = lax.broadcasted_iota(jnp.int32, (tb, tb), 0)
    c = lax.broadcasted_iota(jnp.int32, (tb, tb), 1)
    tri = jnp.where(r > c, 1.0, 0.0).astype(BF16)
    kern = functools.partial(_attn_kernel, tb=tb)
    hw = ATTN_HEADS * SB_HEAD_DIM
    assert q_col % ATTN_HEADS == 0 and SB_HEADS % ATTN_HEADS == 0
    qc, kc, vc = q_col // ATTN_HEADS, k_col // ATTN_HEADS, v_col // ATTN_HEADS
    return pl.pallas_call(
        kern,
        grid=(bsz, SB_HEADS // ATTN_HEADS, nq),
        in_specs=[
            pl.BlockSpec((tb, hw), lambda b, h, i: (b * nq + i, qc + h)),
            pl.BlockSpec((seq, hw), lambda b, h, i: (b, kc + h)),
            pl.BlockSpec((seq, hw), lambda b, h, i: (b, vc + h)),
            pl.BlockSpec((tb, tb), lambda b, h, i: (0, 0)),
        ],
        out_specs=pl.BlockSpec((tb, hw), lambda b, h, i: (b * nq + i, h)),
        out_shape=jax.ShapeDtypeStruct((bsz * seq, SB_HEADS * SB_HEAD_DIM), BF16),
        scratch_shapes=[pltpu.VMEM((tb, hw), F32), pltpu.VMEM((ATTN_HEADS, tb, 1), F32)],
        compiler_params=_params(("parallel", "parallel", "arbitrary")),
        name="attn",
    )(proj, proj, proj, tri)


def _mix_kernel(y_ref, att_ref, gs_ref, ga_ref, x_ref, mod_ref, wglu_ref, wup_ref, wout_ref, g2_ref,
                x1_ref, h2_ref, h2t_ref, *, d):
    yg = jnp.dot(y_ref[...], wglu_ref[...], preferred_element_type=F32)
    ssm = yg[:, :d] * jax.nn.sigmoid(yg[:, d:])
    up = jnp.dot(att_ref[...], wup_ref[...], preferred_element_type=F32)
    merged = (jax.nn.sigmoid(gs_ref[...].astype(F32)) * ssm
              + jax.nn.sigmoid(ga_ref[...].astype(F32)) * up)
    o = jnp.dot(merged.astype(BF16), wout_ref[...], preferred_element_type=F32)
    x1 = x_ref[...] + mod_ref[0, 2:3, :] * o
    x1_ref[...] = x1
    ms = jnp.mean(x1 * x1, axis=-1, keepdims=True)
    h2 = x1 * lax.rsqrt(ms + EPS) * g2_ref[...]
    h2 = h2 * (1.0 + mod_ref[0, 4:5, :]) + mod_ref[0, 3:4, :]
    h2_ref[...] = h2.astype(BF16)
    h2t_ref[...] = h2.T.astype(BF16)


def _mix(ygelu, att, proj, x2, mod3, wglu, wup, wout, g2, seq, gs_col, ga_col):
    t, d = x2.shape
    w = ygelu.shape[1]
    tm = _pick(seq, 256)
    spb = seq // tm
    kern = functools.partial(_mix_kernel, d=d)
    const = lambda a: pl.BlockSpec(a.shape, lambda i: (0,) * a.ndim)
    return pl.pallas_call(
        kern,
        grid=(t // tm,),
        in_specs=[
            pl.BlockSpec((tm, w), lambda i: (i, 0)),
            pl.BlockSpec((tm, w), lambda i: (i, 0)),
            pl.BlockSpec((tm, d), lambda i: (i, gs_col)),
            pl.BlockSpec((tm, d), lambda i: (i, ga_col)),
            pl.BlockSpec((tm, d), lambda i: (i, 0)),
            pl.BlockSpec((1, N_ADA, d), lambda i: (i // spb, 0, 0)),
            const(wglu), const(wup), const(wout), const(g2),
        ],
        out_specs=[
            pl.BlockSpec((tm, d), lambda i: (i, 0)),
            pl.BlockSpec((tm, d), lambda i: (i, 0)),
            pl.BlockSpec((d, tm), lambda i: (0, i)),
        ],
        out_shape=[jax.ShapeDtypeStruct((t, d), F32),
                   jax.ShapeDtypeStruct((t, d), BF16),
                   jax.ShapeDtypeStruct((d, t), BF16)],
        compiler_params=_params(("parallel",)),
        name="mix",
    )(ygelu, att, proj, proj, x2, mod3, wglu, wup, wout, g2)


PEER_RANKS = PEER_TOPK + 1
PEER_RANK_ROWS = 24


def _candidate_tiles(v1_scr, v2_scr, tm):
    tiles = []
    row8 = lax.broadcasted_iota(jnp.int32, (SUBLANES, tm), 0)
    for b in range(PEER_RANKS):
        n_a = PEER_RANKS // (b + 1)
        v2b = v2_scr[b:b + 1, :]
        for base in range(0, n_a, SUBLANES):
            t = v1_scr[base:base + SUBLANES, :] + v2b
            if n_a - base < SUBLANES:
                t = jnp.where(row8 < (n_a - base), t, NEG_INF)
            tiles.append(t)
    return tiles


def _sort_network(n):
    pairs = []
    p = 1
    while p < n:
        k = p
        while k >= 1:
            for j in range(k % p, n - k, 2 * k):
                for i in range(min(k, n - j - k)):
                    if (i + j) // (2 * p) == (i + j + k) // (2 * p):
                        pairs.append((i + j, i + j + k))
            k //= 2
        p *= 2
    return pairs


def _top_values(s, v_scr):
    n_tiles = s.shape[0] // SUBLANES
    rows = [s[k * SUBLANES:(k + 1) * SUBLANES, :] for k in range(n_tiles)]
    for i, j in _sort_network(n_tiles):
        rows[i], rows[j] = jnp.maximum(rows[i], rows[j]), jnp.minimum(rows[i], rows[j])
    for a in range(PEER_RANKS):
        m = jnp.max(rows[0], axis=0, keepdims=True)
        v_scr[a:a + 1, :] = m
        hit = rows[0] >= m
        for k in range(min(n_tiles, PEER_RANKS - 1 - a)):
            below = rows[k + 1] if k + 1 < n_tiles else NEG_INF
            rows[k] = jnp.where(hit, below, rows[k])


def _route_kernel(h2_ref, wq_ref, k1_ref, k2_ref, c1_ref, w1_ref, e2_ref,
                  s_scr, v1_scr, v2_scr, *, tm):
    q = jnp.dot(h2_ref[...], wq_ref[...], preferred_element_type=F32).astype(k1_ref.dtype)
    v1_scr[...] = jnp.full(v1_scr.shape, NEG_INF, F32)
    v2_scr[...] = jnp.full(v2_scr.shape, NEG_INF, F32)
    for h in range(PEER_HEADS):
        for half, (k_ref, v_scr) in enumerate(((k1_ref, v1_scr), (k2_ref, v2_scr))):
            col = (2 * h + half) * PEER_HALF
            qh = q[:, col:col + PEER_HALF]
            s = lax.dot_general(k_ref[...], qh, (((1,), (1,)), ((), ())), preferred_element_type=F32)
            s_scr[half] = s
            _top_values(s, v_scr)
        tiles = _candidate_tiles(v1_scr, v2_scr, tm)
        work = list(tiles)
        ranked = []
        for a in range(PEER_RANKS):
            m = work[0]
            for t in work[1:]:
                m = jnp.maximum(m, t)
            m = jnp.max(m, axis=0, keepdims=True)
            ranked.append(m)
            work = [jnp.where(t >= m, NEG_INF, t) for t in work]
        top = ranked[0]
        thr = 0.5 * (ranked[PEER_TOPK - 1] + ranked[PEER_TOPK])
        zsum = jnp.zeros((SUBLANES, tm), F32)
        for t in tiles:
            zsum = zsum + jnp.where(t >= thr, jnp.exp(t - top), 0.0)
        inv_z = 1.0 / jnp.sum(zsum, axis=0, keepdims=True)
        s1 = s_scr[0]
        s2 = s_scr[1]
        top2 = v2_scr[0:1, :]
        c1_ref[h] = jnp.exp(thr - s1 - top2)
        w1_ref[h] = jnp.exp(s1 - v1_scr[0:1, :]) * inv_z
        e2_ref[h] = jnp.exp(s2 - top2)


def _route(h2, wq, k1, k2):
    t, d = h2.shape
    tm = _pick(t, 256)
    kern = functools.partial(_route_kernel, tm=tm)
    const = lambda a: pl.BlockSpec(a.shape, lambda i: (0,) * a.ndim)
    out_spec = pl.BlockSpec((PEER_HEADS, PEER_N_KEYS, tm), lambda i: (0, 0, i))
    out_sds = jax.ShapeDtypeStruct((PEER_HEADS, PEER_N_KEYS, t), F32)
    return pl.pallas_call(
        kern,
        grid=(t // tm,),
        in_specs=[pl.BlockSpec((tm, d), lambda i: (i, 0)), const(wq), const(k1), const(k2)],
        out_specs=[out_spec] * 3,
        out_shape=[out_sds] * 3,
        scratch_shapes=[pltpu.VMEM((2, PEER_N_KEYS, tm), F32),
                        pltpu.VMEM((PEER_RANK_ROWS, tm), F32),
                        pltpu.VMEM((PEER_RANK_ROWS, tm), F32)],
        compiler_params=_params(("parallel",)),
        name="route",
    )(h2, wq, k1, k2)


EXPERT_TILE = 1024
GATE_ROWS = 32


def _expert_gated(i_first, n_i, row0, act_r, c1_ref, w1_ref, e2_ref, tm):
    ps = []
    for ii in range(n_i):
        i = i_first + ii
        rows = slice(row0 + ii * PEER_N_KEYS, row0 + (ii + 1) * PEER_N_KEYS)
        c_rows = [c1_ref[h, pl.ds(i, 1), :] for h in range(PEER_HEADS)]
        w_rows = [w1_ref[h, pl.ds(i, 1), :] for h in range(PEER_HEADS)]
        chunks = []
        for r0 in range(0, PEER_N_KEYS, GATE_ROWS):
            gate = None
            for h in range(PEER_HEADS):
                e2 = e2_ref[h, r0:r0 + GATE_ROWS, :]
                term = jnp.where(e2 >= c_rows[h], e2 * w_rows[h], 0.0)
                gate = term if gate is None else gate + term
            act = act_r[rows.start + r0:rows.start + r0 + GATE_ROWS, :]
            chunks.append((gate * _gelu(act)).astype(BF16))
        ps.append(jnp.concatenate(chunks, axis=0))
    return ps[0] if len(ps) == 1 else jnp.concatenate(ps, axis=0)


def _experts_kernel(u_ref, h2t_ref, vt_ref, c1_ref, w1_ref, e2_ref, x1_ref, mod_ref, o_ref,
                    acc_ref, act0, act1, *, te, tm, ne):
    e = pl.program_id(1)
    n_i = te // PEER_N_KEYS
    slots = (act0, act1)

    def step(act_w, act_r, i0):
        if act_w is not None:
            act_w[...] = jnp.dot(u_ref[...], h2t_ref[...], preferred_element_type=F32)
        if act_r is not None:
            p = _expert_gated(i0, n_i, 0, act_r, c1_ref, w1_ref, e2_ref, tm)
            acc_ref[...] += jnp.dot(vt_ref[0], p, preferred_element_type=F32)

    @pl.when(e == 0)
    def _():
        acc_ref[...] = jnp.zeros_like(acc_ref)
        step(slots[0], None, None)

    for parity in range(2):
        @pl.when(jnp.logical_and(jnp.logical_and(e > 0, e < ne), lax.rem(e, 2) == parity))
        def _(parity=parity):
            step(slots[parity], slots[1 - parity], (e - 1) * n_i)

    @pl.when(e == ne)
    def _():
        step(None, slots[(ne - 1) % 2], (ne - 1) * n_i)
        o_ref[...] = x1_ref[...] + mod_ref[0, 5:6, :] * acc_ref[...].T


def _experts(u_tab, h2t, v_tab, route_out, x1, mod3, seq):
    n_e, d = u_tab.shape
    t = h2t.shape[1]
    tm = _pick(seq, 512)
    te = _pick(n_e, EXPERT_TILE)
    ne = n_e // te
    spb = seq // tm
    vt_tiles = v_tab.reshape(ne, te, d).transpose(0, 2, 1).astype(BF16)
    kern = functools.partial(_experts_kernel, te=te, tm=tm, ne=ne)
    rspec = pl.BlockSpec((PEER_HEADS, PEER_N_KEYS, tm), lambda i, e: (0, 0, i))
    return pl.pallas_call(
        kern,
        grid=(t // tm, ne + 1),
        in_specs=[
            pl.BlockSpec((te, d), lambda i, e: (jnp.minimum(e, ne - 1), 0)),
            pl.BlockSpec((d, tm), lambda i, e: (0, i)),
            pl.BlockSpec((1, d, te), lambda i, e: (jnp.maximum(e - 1, 0), 0, 0)),
            rspec, rspec, rspec,
            pl.BlockSpec((tm, d), lambda i, e: (i, 0), pipeline_mode=pl.Buffered(1)),
            pl.BlockSpec((1, N_ADA, d), lambda i, e: (i // spb, 0, 0)),
        ],
        out_specs=pl.BlockSpec((tm, d), lambda i, e: (i, 0), pipeline_mode=pl.Buffered(1)),
        out_shape=jax.ShapeDtypeStruct((t, d), F32),
        scratch_shapes=[pltpu.VMEM((d, tm), F32), pltpu.VMEM((te, tm), F32), pltpu.VMEM((te, tm), F32)],
        compiler_params=_params(("parallel", "arbitrary")),
        name="experts",
    )(u_tab, h2t, vt_tiles, *route_out, x1, mod3)


def kernel(x, c, w_ada, b_ada, norm1_g, w_in, lam_re, lam_im, log_dt, ssm_b_re, ssm_b_im, ssm_c_re, ssm_c_im, ssm_d, w_glu, q_norm_g, k_norm_g, w_att_up, w_out, norm2_g, peer_wq, peer_k1, peer_k2, peer_u, peer_v):
    bsz, seq, d = x.shape
    depth = w_ada.shape[0]
    ssm_width = ssm_d.shape[1] * ssm_d.shape[2]
    sb_width = SB_HEADS * SB_HEAD_DIM
    assert ssm_width == sb_width and d % sb_width == 0
    assert ssm_d.shape[1] % S5_CHUNKS == 0 and ssm_d.shape[2] == SSM_GROUP and lam_re.shape[2] == SSM_STATE
    q_col = sb_width // SB_HEAD_DIM
    k_col = 2 * q_col
    v_col = 3 * q_col
    gate_base = 4 * sb_width
    assert gate_base % d == 0
    gs_col = gate_base // d
    ga_col = gs_col + 1

    x2 = x.reshape(bsz * seq, d)
    c_pad = jnp.zeros((SUBLANES, d), F32).at[:bsz].set(c)
    for l in range(depth):
        mod = _ada(c_pad, w_ada[l], b_ada[l].reshape(1, -1))
        mod3 = mod.reshape(SUBLANES, N_ADA, d)
        proj = _inproj(x2, mod3, norm1_g[l].reshape(1, d), w_in[l].astype(BF16),
                       q_norm_g[l].reshape(1, -1), k_norm_g[l].reshape(1, -1), seq, sb_width)
        s5_ops = _s5_operands(lam_re[l], lam_im[l], log_dt[l], ssm_b_re[l], ssm_b_im[l],
                              ssm_c_re[l], ssm_c_im[l], ssm_d[l])
        ygelu = _s5(proj, s5_ops, bsz, seq, ssm_width)
        att = _attn(proj, bsz, seq, q_col, k_col, v_col)
        x1, h2, h2t = _mix(ygelu, att, proj, x2, mod3, w_glu[l].astype(BF16), w_att_up[l].astype(BF16),
                           w_out[l].astype(BF16), norm2_g[l].reshape(1, d), seq, gs_col, ga_col)
        route_out = _route(h2, peer_wq[l].astype(BF16), peer_k1[l].astype(BF16), peer_k2[l].astype(BF16))
        x2 = _experts(peer_u[l].astype(BF16), h2t, peer_v[l], route_out, x1, mod3, seq)
    return x2.reshape(bsz, seq, d)
```

```python
import functools
import math

import jax
import jax.numpy as jnp
from jax import lax
from jax.experimental import pallas as pl
from jax.experimental.pallas import tpu as pltpu

F32 = jnp.float32
BF16 = jnp.bfloat16

LANES = 128
SUBLANES = 8
VMEM_LIMIT = 56 * 1024 * 1024

SSM_GROUP = 16
SSM_STATE = 64
SB_HEADS = 8
SB_HEAD_DIM = 128
PEER_HEADS = 8
PEER_N_KEYS = 128
PEER_TOPK = 16
PEER_HALF = 128
N_ADA = 6
EPS = 1e-6
NEG_INF = float("-inf")
LOG2E = math.log2(math.e)


def _params(sem):
    return pltpu.CompilerParams(dimension_semantics=sem, vmem_limit_bytes=VMEM_LIMIT)


GELU_C = -2.0 * math.sqrt(2.0 / math.pi) * LOG2E


def _gelu(x):
    return x / (1.0 + jnp.exp2(x * (GELU_C + (GELU_C * 0.044715) * (x * x))))


def _pick(n, pref):
    t = min(n, pref)
    while n % t:
        t //= 2
    return t


def _ada_kernel(c_ref, w_ref, b_ref, o_ref):
    c = c_ref[...]
    c = c * jax.nn.sigmoid(c)
    o_ref[...] = jnp.dot(c, w_ref[...], preferred_element_type=F32) + b_ref[...]


def _ada(c_pad, w, b):
    rows, d = c_pad.shape
    n = w.shape[1]
    tn = _pick(n, 1024)
    return pl.pallas_call(
        _ada_kernel,
        grid=(n // tn,),
        in_specs=[
            pl.BlockSpec((rows, d), lambda j: (0, 0)),
            pl.BlockSpec((d, tn), lambda j: (0, j)),
            pl.BlockSpec((1, tn), lambda j: (0, j)),
        ],
        out_specs=pl.BlockSpec((rows, tn), lambda j: (0, j)),
        out_shape=jax.ShapeDtypeStruct((rows, n), F32),
        compiler_params=_params(("arbitrary",)),
        name="ada",
    )(c_pad, w, b)


def _head_rms(acc, g_ref, o_ref, scale):
    for h in range(acc.shape[1] // SB_HEAD_DIM):
        blk = acc[:, h * SB_HEAD_DIM:(h + 1) * SB_HEAD_DIM]
        ms = jnp.mean(blk * blk, axis=-1, keepdims=True)
        y = blk * lax.rsqrt(ms + EPS) * g_ref[...]
        if scale != 1.0:
            y = y * scale
        o_ref[:, h * SB_HEAD_DIM:(h + 1) * SB_HEAD_DIM] = y.astype(o_ref.dtype)


def _inproj_kernel(x_ref, mod_ref, g_ref, w_ref, qg_ref, kg_ref, o_ref, h_scr, *, q_blk, k_blk):
    j = pl.program_id(1)

    @pl.when(j == 0)
    def _():
        x = x_ref[...]
        ms = jnp.mean(x * x, axis=-1, keepdims=True)
        y = x * lax.rsqrt(ms + EPS) * g_ref[...]
        h = y * (1.0 + mod_ref[0, 1:2, :]) + mod_ref[0, 0:1, :]
        h_scr[...] = h.astype(BF16)

    acc = jnp.dot(h_scr[...], w_ref[...], preferred_element_type=F32)

    @pl.when(j == q_blk)
    def _():
        _head_rms(acc, qg_ref, o_ref, SB_HEAD_DIM ** -0.5 * LOG2E)

    @pl.when(j == k_blk)
    def _():
        _head_rms(acc, kg_ref, o_ref, 1.0)

    @pl.when(jnp.logical_and(j != q_blk, j != k_blk))
    def _():
        o_ref[...] = acc.astype(o_ref.dtype)


def _inproj(x2, mod3, g, w, qg, kg, seq, sb_width):
    t, d = x2.shape
    n = w.shape[1]
    tm = _pick(seq, 1024)
    tn = sb_width
    steps_per_batch = seq // tm
    kern = functools.partial(_inproj_kernel, q_blk=1, k_blk=2)
    return pl.pallas_call(
        kern,
        grid=(t // tm, n // tn),
        in_specs=[
            pl.BlockSpec((tm, d), lambda i, j: (i, 0)),
            pl.BlockSpec((1, N_ADA, d), lambda i, j: (i // steps_per_batch, 0, 0)),
            pl.BlockSpec((1, d), lambda i, j: (0, 0)),
            pl.BlockSpec((d, tn), lambda i, j: (0, j)),
            pl.BlockSpec((1, SB_HEAD_DIM), lambda i, j: (0, 0)),
            pl.BlockSpec((1, SB_HEAD_DIM), lambda i, j: (0, 0)),
        ],
        out_specs=pl.BlockSpec((tm, tn), lambda i, j: (i, j)),
        out_shape=jax.ShapeDtypeStruct((t, n), BF16),
        scratch_shapes=[pltpu.VMEM((tm, d), BF16)],
        compiler_params=_params(("parallel", "arbitrary")),
        name="inproj",
    )(x2, mod3, g, w, qg, kg)


S5_CHUNKS = 8
S5_CHUNK_IN = LANES
S5_CHUNK_STATE = 8 * SSM_STATE
S5_SLABS = S5_CHUNK_STATE // LANES
S5_PITCH_PAD = SUBLANES


def _s5_kernel(u_ref, br_ref, bi_ref, cr_ref, ci_ref, lamr_ref, lami_ref, d_ref, o_ref,
               bur, bui, xrs, xis, sr, si, *, tm, pitch):
    @pl.when(pl.program_id(1) == 0)
    def _():
        sr[...] = jnp.zeros_like(sr)
        si[...] = jnp.zeros_like(si)

    for c in range(S5_CHUNKS):
        uc = u_ref[:, c * S5_CHUNK_IN:(c + 1) * S5_CHUNK_IN]
        r = jnp.dot(uc, br_ref[c], preferred_element_type=F32)
        im = jnp.dot(uc, bi_ref[c], preferred_element_type=F32)
        for l in range(S5_SLABS):
            bur[l, c * pitch:c * pitch + tm, :] = r[:, l * LANES:(l + 1) * LANES]
            bui[l, c * pitch:c * pitch + tm, :] = im[:, l * LANES:(l + 1) * LANES]

    lamr = [lamr_ref[:, l * LANES:(l + 1) * LANES] for l in range(S5_SLABS)]
    lami = [lami_ref[:, l * LANES:(l + 1) * LANES] for l in range(S5_SLABS)]

    def body(t, carry):
        xr, xi = carry
        nxr, nxi = [], []
        for l in range(S5_SLABS):
            rows = pl.ds(t, S5_CHUNKS, stride=pitch)
            b_r = bur[l, rows, :]
            b_i = bui[l, rows, :]
            n_r = lamr[l] * xr[l] - lami[l] * xi[l] + b_r
            n_i = lamr[l] * xi[l] + lami[l] * xr[l] + b_i
            xrs[l, rows, :] = n_r
            xis[l, rows, :] = n_i
            nxr.append(n_r)
            nxi.append(n_i)
        return tuple(nxr), tuple(nxi)

    init = (tuple(sr[:, l * LANES:(l + 1) * LANES] for l in range(S5_SLABS)),
            tuple(si[:, l * LANES:(l + 1) * LANES] for l in range(S5_SLABS)))
    xr, xi = lax.fori_loop(0, tm, body, init, unroll=8)
    for l in range(S5_SLABS):
        sr[:, l * LANES:(l + 1) * LANES] = xr[l]
        si[:, l * LANES:(l + 1) * LANES] = xi[l]

    for c in range(S5_CHUNKS):
        acc = jnp.zeros((tm, S5_CHUNK_IN), F32)
        for l in range(S5_SLABS):
            xr_c = xrs[l, c * pitch:c * pitch + tm, :].astype(BF16)
            xi_c = xis[l, c * pitch:c * pitch + tm, :].astype(BF16)
            acc = acc + jnp.dot(xr_c, cr_ref[c, l * LANES:(l + 1) * LANES, :], preferred_element_type=F32)
            acc = acc + jnp.dot(xi_c, ci_ref[c, l * LANES:(l + 1) * LANES, :], preferred_element_type=F32)
        cols = slice(c * S5_CHUNK_IN, (c + 1) * S5_CHUNK_IN)
        y = acc + d_ref[:, cols] * u_ref[:, cols].astype(F32)
        o_ref[:, cols] = _gelu(y).astype(o_ref.dtype)


def _s5_operands(lam_re, lam_im, log_dt, b_re, b_im, c_re, c_im, d_skip):
    g, p = lam_re.shape
    i_dim = b_re.shape[-1]
    dt = jnp.exp(log_dt)[:, None]
    mag = jnp.exp(lam_re * dt)
    lbr = mag * jnp.cos(lam_im * dt)
    lbi = mag * jnp.sin(lam_im * dt)
    nr, ni = lbr - 1.0, lbi
    den = lam_re * lam_re + lam_im * lam_im
    cfr = (nr * lam_re + ni * lam_im) / den
    cfi = (ni * lam_re - nr * lam_im) / den
    bbr = cfr[..., None] * b_re - cfi[..., None] * b_im
    bbi = cfr[..., None] * b_im + cfi[..., None] * b_re
    gpc = g // S5_CHUNKS
    eye = jnp.eye(gpc, dtype=F32)

    def pack_b(m):
        m = m.reshape(S5_CHUNKS, gpc, p, i_dim)
        return jnp.einsum("cgpi,gh->cgihp", m, eye).reshape(S5_CHUNKS, gpc * i_dim, gpc * p)

    def pack_c(m):
        m = m.reshape(S5_CHUNKS, gpc, i_dim, p)
        return jnp.einsum("cgip,gh->cgphi", m, eye).reshape(S5_CHUNKS, gpc * p, gpc * i_dim)

    return (pack_b(bbr).astype(BF16), pack_b(bbi).astype(BF16),
            pack_c(c_re).astype(BF16), pack_c(-c_im).astype(BF16),
            lbr.reshape(S5_CHUNKS, gpc * p), lbi.reshape(S5_CHUNKS, gpc * p),
            d_skip.reshape(1, g * i_dim))


def _s5(proj, ops, bsz, seq, width):
    br, bi, cr, ci, lamr, lami, dsk = ops
    tm = _pick(seq, 256)
    pitch = tm + S5_PITCH_PAD
    nt = seq // tm
    kern = functools.partial(_s5_kernel, tm=tm, pitch=pitch)
    full = lambda a: pl.BlockSpec(a.shape, lambda b, i: (0,) * a.ndim)
    scr = pltpu.VMEM((S5_SLABS, S5_CHUNKS * pitch, LANES), F32)
    return pl.pallas_call(
        kern,
        grid=(bsz, nt),
        in_specs=[pl.BlockSpec((tm, width), lambda b, i: (b * nt + i, 0)),
                  full(br), full(bi), full(cr), full(ci), full(lamr), full(lami), full(dsk)],
        out_specs=pl.BlockSpec((tm, width), lambda b, i: (b * nt + i, 0)),
        out_shape=jax.ShapeDtypeStruct((bsz * seq, width), BF16),
        scratch_shapes=[scr, scr, scr, scr,
                        pltpu.VMEM((S5_CHUNKS, S5_CHUNK_STATE), F32),
                        pltpu.VMEM((S5_CHUNKS, S5_CHUNK_STATE), F32)],
        compiler_params=_params(("parallel", "arbitrary")),
        name="s5",
    )(proj, br, bi, cr, ci, lamr, lami, dsk)


ATTN_GROUP = 4
ATTN_HEADS = 2
ATTN_DEAD_CARRY = 160.0


def _neg_abs(z):
    bits = lax.bitcast_convert_type(z, jnp.uint32) | jnp.uint32(0x80000000)
    return lax.bitcast_convert_type(bits, F32)


def _attn_kernel(q_ref, k_ref, v_ref, tri_ref, o_ref, acc_ref, carry_ref, *, tb):
    qi = pl.program_id(2)
    acc_ref[...] = jnp.zeros_like(acc_ref)
    carry_ref[...] = jnp.zeros_like(carry_ref)

    def group(kb_lo, n, diagonal):
        ks = pl.multiple_of(kb_lo * tb, tb)
        for hh in range(ATTN_HEADS):
            cols = slice(hh * SB_HEAD_DIM, (hh + 1) * SB_HEAD_DIM)
            q = q_ref[:, cols]
            k = k_ref[pl.ds(ks, n * tb), cols]
            v = v_ref[pl.ds(ks, n * tb), cols]
            carry = carry_ref[hh]
            ws = [None] * n
            for g in reversed(range(n)):
                z = lax.dot_general(q, k[g * tb:(g + 1) * tb, :], (((1,), (1,)), ((), ())),
                                    preferred_element_type=F32)
                sp = jnp.maximum(z, 0.0) + jnp.log(1.0 + jnp.exp2(_neg_abs(z))) * LOG2E
                masked = diagonal and g == n - 1
                if masked:
                    row = lax.broadcasted_iota(jnp.int32, (tb, tb), 0)
                    col = lax.broadcasted_iota(jnp.int32, (tb, tb), 1)
                    causal = col < row
                    sp_sum = jnp.where(causal, sp, 0.0)
                else:
                    sp_sum = sp
                cum = jnp.dot(sp_sum.astype(BF16), tri_ref[...], preferred_element_type=F32)
                w = jnp.exp2(z - sp - cum - carry)
                if masked:
                    w = jnp.where(causal, w, 0.0)
                ws[g] = w.astype(BF16)
                carry = carry + jnp.sum(sp_sum, axis=1, keepdims=True)
            w_all = ws[0] if n == 1 else jnp.concatenate(ws, axis=1)
            acc_ref[:, cols] += jnp.dot(w_all, v, preferred_element_type=F32)
            carry_ref[hh] = carry

    def live():
        return (jnp.min(carry_ref[...]) < ATTN_DEAD_CARRY).astype(jnp.int32)

    group(qi, 1, True)
    done = jnp.int32(0)
    size = 1
    while size < ATTN_GROUP:
        bit = (qi & size) != 0

        @pl.when(jnp.logical_and(bit, live() > 0))
        def _(done=done, size=size):
            group(qi - done - size, size, False)

        done = done + jnp.where(bit, size, 0)
        size *= 2

    n_full = qi // ATTN_GROUP

    def more(state):
        it, alive = state
        return jnp.logical_and(it < n_full, alive > 0)

    def multi(state):
        it, _ = state
        group(qi - done - (it + 1) * ATTN_GROUP, ATTN_GROUP, False)
        return it + 1, live()

    lax.while_loop(more, multi, (jnp.int32(0), live()))
    o_ref[...] = acc_ref[...].astype(o_ref.dtype)


def _attn(proj, bsz, seq, q_col, k_col, v_col):
    tb = _pick(seq, 256)
    nq = seq // tb
    r = lax.broadcasted_iota(jnp.int32, (tb, tb), 0)
    c = lax.broadcasted_iota(jnp.int32, (tb, tb), 1)
    tri = jnp.where(r > c, 1.0, 0.0).astype(BF16)
    kern = functools.partial(_attn_kernel, tb=tb)
    hw = ATTN_HEADS * SB_HEAD_DIM
    assert q_col % ATTN_HEADS == 0 and SB_HEADS % ATTN_HEADS == 0
    qc, kc, vc = q_col // ATTN_HEADS, k_col // ATTN_HEADS, v_col // ATTN_HEADS
    return pl.pallas_call(
        kern,
        grid=(bsz, SB_HEADS // ATTN_HEADS, nq),
        in_specs=[
            pl.BlockSpec((tb, hw), lambda b, h, i: (b * nq + i, qc + h)),
            pl.BlockSpec((seq, hw), lambda b, h, i: (b, kc + h)),
            pl.BlockSpec((seq, hw), lambda b, h, i: (b, vc + h)),
            pl.BlockSpec((tb, tb), lambda b, h, i: (0, 0)),
        ],
        out_specs=pl.BlockSpec((tb, hw), lambda b, h, i: (b * nq + i, h)),
        out_shape=jax.ShapeDtypeStruct((bsz * seq, SB_HEADS * SB_HEAD_DIM), BF16),
        scratch_shapes=[pltpu.VMEM((tb, hw), F32), pltpu.VMEM((ATTN_HEADS, tb, 1), F32)],
        compiler_params=_params(("parallel", "parallel", "arbitrary")),
        name="attn",
    )(proj, proj, proj, tri)


def _mix_kernel(y_ref, att_ref, gs_ref, ga_ref, x_ref, mod_ref, wglu_ref, wup_ref, wout_ref, g2_ref,
                x1_ref, h2_ref, h2t_ref, *, d):
    yg = jnp.dot(y_ref[...], wglu_ref[...], preferred_element_type=F32)
    ssm = yg[:, :d] * jax.nn.sigmoid(yg[:, d:])
    up = jnp.dot(att_ref[...], wup_ref[...], preferred_element_type=F32)
    merged = (jax.nn.sigmoid(gs_ref[...].astype(F32)) * ssm
              + jax.nn.sigmoid(ga_ref[...].astype(F32)) * up)
    o = jnp.dot(merged.astype(BF16), wout_ref[...], preferred_element_type=F32)
    x1 = x_ref[...] + mod_ref[0, 2:3, :] * o
    x1_ref[...] = x1
    ms = jnp.mean(x1 * x1, axis=-1, keepdims=True)
    h2 = x1 * lax.rsqrt(ms + EPS) * g2_ref[...]
    h2 = h2 * (1.0 + mod_ref[0, 4:5, :]) + mod_ref[0, 3:4, :]
    h2_ref[...] = h2.astype(BF16)
    h2t_ref[...] = h2.T.astype(BF16)


def _mix(ygelu, att, proj, x2, mod3, wglu, wup, wout, g2, seq, gs_col, ga_col):
    t, d = x2.shape
    w = ygelu.shape[1]
    tm = _pick(seq, 256)
    spb = seq // tm
    kern = functools.partial(_mix_kernel, d=d)
    const = lambda a: pl.BlockSpec(a.shape, lambda i: (0,) * a.ndim)
    return pl.pallas_call(
        kern,
        grid=(t // tm,),
        in_specs=[
            pl.BlockSpec((tm, w), lambda i: (i, 0)),
            pl.BlockSpec((tm, w), lambda i: (i, 0)),
            pl.BlockSpec((tm, d), lambda i: (i, gs_col)),
            pl.BlockSpec((tm, d), lambda i: (i, ga_col)),
            pl.BlockSpec((tm, d), lambda i: (i, 0)),
            pl.BlockSpec((1, N_ADA, d), lambda i: (i // spb, 0, 0)),
            const(wglu), const(wup), const(wout), const(g2),
        ],
        out_specs=[
            pl.BlockSpec((tm, d), lambda i: (i, 0)),
            pl.BlockSpec((tm, d), lambda i: (i, 0)),
            pl.BlockSpec((d, tm), lambda i: (0, i)),
        ],
        out_shape=[jax.ShapeDtypeStruct((t, d), F32),
                   jax.ShapeDtypeStruct((t, d), BF16),
                   jax.ShapeDtypeStruct((d, t), BF16)],
        compiler_params=_params(("parallel",)),
        name="mix",
    )(ygelu, att, proj, proj, x2, mod3, wglu, wup, wout, g2)


PEER_RANKS = PEER_TOPK + 1
PEER_RANK_ROWS = 24


def _candidate_tiles(v1_scr, v2_scr, tm):
    tiles = []
    row8 = lax.broadcasted_iota(jnp.int32, (SUBLANES, tm), 0)
    for b in range(PEER_RANKS):
        n_a = PEER_RANKS // (b + 1)
        v2b = v2_scr[b:b + 1, :]
        for base in range(0, n_a, SUBLANES):
            t = v1_scr[base:base + SUBLANES, :] + v2b
            if n_a - base < SUBLANES:
                t = jnp.where(row8 < (n_a - base), t, NEG_INF)
            tiles.append(t)
    return tiles


def _sort_network(n):
    pairs = []
    p = 1
    while p < n:
        k = p
        while k >= 1:
            for j in range(k % p, n - k, 2 * k):
                for i in range(min(k, n - j - k)):
                    if (i + j) // (2 * p) == (i + j + k) // (2 * p):
                        pairs.append((i + j, i + j + k))
            k //= 2
        p *= 2
    return pairs


def _top_values(s, v_scr):
    n_tiles = s.shape[0] // SUBLANES
    rows = [s[k * SUBLANES:(k + 1) * SUBLANES, :] for k in range(n_tiles)]
    for i, j in _sort_network(n_tiles):
        rows[i], rows[j] = jnp.maximum(rows[i], rows[j]), jnp.minimum(rows[i], rows[j])
    for a in range(PEER_RANKS):
        m = jnp.max(rows[0], axis=0, keepdims=True)
        v_scr[a:a + 1, :] = m
        hit = rows[0] >= m
        for k in range(min(n_tiles, PEER_RANKS - 1 - a)):
            below = rows[k + 1] if k + 1 < n_tiles else NEG_INF
            rows[k] = jnp.where(hit, below, rows[k])


def _route_kernel(h2_ref, wq_ref, k1_ref, k2_ref, c1_ref, w1_ref, e2_ref,
                  s_scr, v1_scr, v2_scr, *, tm):
    q = jnp.dot(h2_ref[...], wq_ref[...], preferred_element_type=F32).astype(k1_ref.dtype)
    v1_scr[...] = jnp.full(v1_scr.shape, NEG_INF, F32)
    v2_scr[...] = jnp.full(v2_scr.shape, NEG_INF, F32)
    for h in range(PEER_HEADS):
        for half, (k_ref, v_scr) in enumerate(((k1_ref, v1_scr), (k2_ref, v2_scr))):
            col = (2 * h + half) * PEER_HALF
            qh = q[:, col:col + PEER_HALF]
            s = lax.dot_general(k_ref[...], qh, (((1,), (1,)), ((), ())), preferred_element_type=F32)
            s_scr[half] = s
            _top_values(s, v_scr)
        tiles = _candidate_tiles(v1_scr, v2_scr, tm)
        work = list(tiles)
        ranked = []
        for a in range(PEER_RANKS):
            m = work[0]
            for t in work[1:]:
                m = jnp.maximum(m, t)
            m = jnp.max(m, axis=0, keepdims=True)
            ranked.append(m)
            work = [jnp.where(t >= m, NEG_INF, t) for t in work]
        top = ranked[0]
        thr = 0.5 * (ranked[PEER_TOPK - 1] + ranked[PEER_TOPK])
        zsum = jnp.zeros((SUBLANES, tm), F32)
        for t in tiles:
            zsum = zsum + jnp.where(t >= thr, jnp.exp(t - top), 0.0)
        inv_z = 1.0 / jnp.sum(zsum, axis=0, keepdims=True)
        s1 = s_scr[0]
        s2 = s_scr[1]
        top2 = v2_scr[0:1, :]
        c1_ref[h] = jnp.exp(thr - s1 - top2)
        w1_ref[h] = jnp.exp(s1 - v1_scr[0:1, :]) * inv_z
        e2_ref[h] = jnp.exp(s2 - top2)


def _route(h2, wq, k1, k2):
    t, d = h2.shape
    tm = _pick(t, 256)
    kern = functools.partial(_route_kernel, tm=tm)
    const = lambda a: pl.BlockSpec(a.shape, lambda i: (0,) * a.ndim)
    out_spec = pl.BlockSpec((PEER_HEADS, PEER_N_KEYS, tm), lambda i: (0, 0, i))
    out_sds = jax.ShapeDtypeStruct((PEER_HEADS, PEER_N_KEYS, t), F32)
    return pl.pallas_call(
        kern,
        grid=(t // tm,),
        in_specs=[pl.BlockSpec((tm, d), lambda i: (i, 0)), const(wq), const(k1), const(k2)],
        out_specs=[out_spec] * 3,
        out_shape=[out_sds] * 3,
        scratch_shapes=[pltpu.VMEM((2, PEER_N_KEYS, tm), F32),
                        pltpu.VMEM((PEER_RANK_ROWS, tm), F32),
                        pltpu.VMEM((PEER_RANK_ROWS, tm), F32)],
        compiler_params=_params(("parallel",)),
        name="route",
    )(h2, wq, k1, k2)


EXPERT_TILE = 1024
GATE_ROWS = 32


def _expert_gated(i_first, n_i, row0, act_r, c1_ref, w1_ref, e2_ref, tm):
    ps = []
    for ii in range(n_i):
        i = i_first + ii
        rows = slice(row0 + ii * PEER_N_KEYS, row0 + (ii + 1) * PEER_N_KEYS)
        c_rows = [c1_ref[h, pl.ds(i, 1), :] for h in range(PEER_HEADS)]
        w_rows = [w1_ref[h, pl.ds(i, 1), :] for h in range(PEER_HEADS)]
        chunks = []
        for r0 in range(0, PEER_N_KEYS, GATE_ROWS):
            gate = None
            for h in range(PEER_HEADS):
                e2 = e2_ref[h, r0:r0 + GATE_ROWS, :]
                term = jnp.where(e2 >= c_rows[h], e2 * w_rows[h], 0.0)
                gate = term if gate is None else gate + term
            act = act_r[rows.start + r0:rows.start + r0 + GATE_ROWS, :]
            chunks.append((gate * _gelu(act)).astype(BF16))
        ps.append(jnp.concatenate(chunks, axis=0))
    return ps[0] if len(ps) == 1 else jnp.concatenate(ps, axis=0)


def _experts_kernel(u_ref, h2t_ref, vt_ref, c1_ref, w1_ref, e2_ref, x1_ref, mod_ref, o_ref,
                    acc_ref, act0, act1, *, te, tm, ne):
    e = pl.program_id(1)
    n_i = te // PEER_N_KEYS
    slots = (act0, act1)

    def step(act_w, act_r, i0):
        if act_w is not None:
            act_w[...] = jnp.dot(u_ref[...], h2t_ref[...], preferred_element_type=F32)
        if act_r is not None:
            p = _expert_gated(i0, n_i, 0, act_r, c1_ref, w1_ref, e2_ref, tm)
            acc_ref[...] += jnp.dot(vt_ref[0], p, preferred_element_type=F32)

    @pl.when(e == 0)
    def _():
        acc_ref[...] = jnp.zeros_like(acc_ref)
        step(slots[0], None, None)

    for parity in range(2):
        @pl.when(jnp.logical_and(jnp.logical_and(e > 0, e < ne), lax.rem(e, 2) == parity))
        def _(parity=parity):
            step(slots[parity], slots[1 - parity], (e - 1) * n_i)

    @pl.when(e == ne)
    def _():
        step(None, slots[(ne - 1) % 2], (ne - 1) * n_i)
        o_ref[...] = x1_ref[...] + mod_ref[0, 5:6, :] * acc_ref[...].T


def _experts(u_tab, h2t, v_tab, route_out, x1, mod3, seq):
    n_e, d = u_tab.shape
    t = h2t.shape[1]
    tm = _pick(seq, 512)
    te = _pick(n_e, EXPERT_TILE)
    ne = n_e // te
    spb = seq // tm
    vt_tiles = v_tab.reshape(ne, te, d).transpose(0, 2, 1).astype(BF16)
    kern = functools.partial(_experts_kernel, te=te, tm=tm, ne=ne)
    rspec = pl.BlockSpec((PEER_HEADS, PEER_N_KEYS, tm), lambda i, e: (0, 0, i))
    return pl.pallas_call(
        kern,
        grid=(t // tm, ne + 1),
        in_specs=[
            pl.BlockSpec((te, d), lambda i, e: (jnp.minimum(e, ne - 1), 0)),
            pl.BlockSpec((d, tm), lambda i, e: (0, i)),
            pl.BlockSpec((1, d, te), lambda i, e: (jnp.maximum(e - 1, 0), 0, 0)),
            rspec, rspec, rspec,
            pl.BlockSpec((tm, d), lambda i, e: (i, 0), pipeline_mode=pl.Buffered(1)),
            pl.BlockSpec((1, N_ADA, d), lambda i, e: (i // spb, 0, 0)),
        ],
        out_specs=pl.BlockSpec((tm, d), lambda i, e: (i, 0), pipeline_mode=pl.Buffered(1)),
        out_shape=jax.ShapeDtypeStruct((t, d), F32),
        scratch_shapes=[pltpu.VMEM((d, tm), F32), pltpu.VMEM((te, tm), F32), pltpu.VMEM((te, tm), F32)],
        compiler_params=_params(("parallel", "arbitrary")),
        name="experts",
    )(u_tab, h2t, vt_tiles, *route_out, x1, mod3)


def kernel(x, c, w_ada, b_ada, norm1_g, w_in, lam_re, lam_im, log_dt, ssm_b_re, ssm_b_im, ssm_c_re, ssm_c_im, ssm_d, w_glu, q_norm_g, k_norm_g, w_att_up, w_out, norm2_g, peer_wq, peer_k1, peer_k2, peer_u, peer_v):
    bsz, seq, d = x.shape
    depth = w_ada.shape[0]
    ssm_width = ssm_d.shape[1] * ssm_d.shape[2]
    sb_width = SB_HEADS * SB_HEAD_DIM
    assert ssm_width == sb_width and d % sb_width == 0
    assert ssm_d.shape[1] % S5_CHUNKS == 0 and ssm_d.shape[2] == SSM_GROUP and lam_re.shape[2] == SSM_STATE
    q_col = sb_width // SB_HEAD_DIM
    k_col = 2 * q_col
    v_col = 3 * q_col
    gate_base = 4 * sb_width
    assert gate_base % d == 0
    gs_col = gate_base // d
    ga_col = gs_col + 1

    x2 = x.reshape(bsz * seq, d)
    c_pad = jnp.zeros((SUBLANES, d), F32).at[:bsz].set(c)
    for l in range(depth):
        mod = _ada(c_pad, w_ada[l], b_ada[l].reshape(1, -1))
        mod3 = mod.reshape(SUBLANES, N_ADA, d)
        proj = _inproj(x2, mod3, norm1_g[l].reshape(1, d), w_in[l].astype(BF16),
                       q_norm_g[l].reshape(1, -1), k_norm_g[l].reshape(1, -1), seq, sb_width)
        s5_ops = _s5_operands(lam_re[l], lam_im[l], log_dt[l], ssm_b_re[l], ssm_b_im[l],
                              ssm_c_re[l], ssm_c_im[l], ssm_d[l])
        ygelu = _s5(proj, s5_ops, bsz, seq, ssm_width)
        att = _attn(proj, bsz, seq, q_col, k_col, v_col)
        x1, h2, h2t = _mix(ygelu, att, proj, x2, mod3, w_glu[l].astype(BF16), w_att_up[l].astype(BF16),
                           w_out[l].astype(BF16), norm2_g[l].reshape(1, d), seq, gs_col, ga_col)
        route_out = _route(h2, peer_wq[l].astype(BF16), peer_k1[l].astype(BF16), peer_k2[l].astype(BF16))
        x2 = _experts(peer_u[l].astype(BF16), h2t, peer_v[l], route_out, x1, mod3, seq)
    return x2.reshape(bsz, seq, d)
```

```python
import functools
import math

import jax
import jax.numpy as jnp
from jax import lax
from jax.experimental import pallas as pl
from jax.experimental.pallas import tpu as pltpu

F32 = jnp.float32
BF16 = jnp.bfloat16

LANES = 128
SUBLANES = 8
VMEM_LIMIT = 56 * 1024 * 1024

SSM_GROUP = 16
SSM_STATE = 64
SB_HEADS = 8
SB_HEAD_DIM = 128
PEER_HEADS = 8
PEER_N_KEYS = 128
PEER_TOPK = 16
PEER_HALF = 128
N_ADA = 6
EPS = 1e-6
NEG_INF = float("-inf")
LOG2E = math.log2(math.e)


def _params(sem):
    return pltpu.CompilerParams(dimension_semantics=sem, vmem_limit_bytes=VMEM_LIMIT)


GELU_C = -2.0 * math.sqrt(2.0 / math.pi) * LOG2E


def _gelu(x):
    return x / (1.0 + jnp.exp2(x * (GELU_C + (GELU_C * 0.044715) * (x * x))))


def _pick(n, pref):
    t = min(n, pref)
    while n % t:
        t //= 2
    return t


def _ada_kernel(c_ref, w_ref, b_ref, o_ref):
    c = c_ref[...]
    c = c * jax.nn.sigmoid(c)
    o_ref[...] = jnp.dot(c, w_ref[...], preferred_element_type=F32) + b_ref[...]


def _ada(c_pad, w, b):
    rows, d = c_pad.shape
    n = w.shape[1]
    tn = _pick(n, 1024)
    return pl.pallas_call(
        _ada_kernel,
        grid=(n // tn,),
        in_specs=[
            pl.BlockSpec((rows, d), lambda j: (0, 0)),
            pl.BlockSpec((d, tn), lambda j: (0, j)),
            pl.BlockSpec((1, tn), lambda j: (0, j)),
        ],
        out_specs=pl.BlockSpec((rows, tn), lambda j: (0, j)),
        out_shape=jax.ShapeDtypeStruct((rows, n), F32),
        compiler_params=_params(("arbitrary",)),
        name="ada",
    )(c_pad, w, b)


def _head_rms(acc, g_ref, o_ref, scale):
    for h in range(acc.shape[1] // SB_HEAD_DIM):
        blk = acc[:, h * SB_HEAD_DIM:(h + 1) * SB_HEAD_DIM]
        ms = jnp.mean(blk * blk, axis=-1, keepdims=True)
        y = blk * lax.rsqrt(ms + EPS) * g_ref[...]
        if scale != 1.0:
            y = y * scale
        o_ref[:, h * SB_HEAD_DIM:(h + 1) * SB_HEAD_DIM] = y.astype(o_ref.dtype)


def _inproj_kernel(x_ref, mod_ref, g_ref, w_ref, qg_ref, kg_ref, o_ref, h_scr, *, q_blk, k_blk):
    j = pl.program_id(1)

    @pl.when(j == 0)
    def _():
        x = x_ref[...]
        ms = jnp.mean(x * x, axis=-1, keepdims=True)
        y = x * lax.rsqrt(ms + EPS) * g_ref[...]
        h = y * (1.0 + mod_ref[0, 1:2, :]) + mod_ref[0, 0:1, :]
        h_scr[...] = h.astype(BF16)

    acc = jnp.dot(h_scr[...], w_ref[...], preferred_element_type=F32)

    @pl.when(j == q_blk)
    def _():
        _head_rms(acc, qg_ref, o_ref, SB_HEAD_DIM ** -0.5 * LOG2E)

    @pl.when(j == k_blk)
    def _():
        _head_rms(acc, kg_ref, o_ref, 1.0)

    @pl.when(jnp.logical_and(j != q_blk, j != k_blk))
    def _():
        o_ref[...] = acc.astype(o_ref.dtype)


def _inproj(x2, mod3, g, w, qg, kg, seq, sb_width):
    t, d = x2.shape
    n = w.shape[1]
    tm = _pick(seq, 1024)
    tn = sb_width
    steps_per_batch = seq // tm
    kern = functools.partial(_inproj_kernel, q_blk=1, k_blk=2)
    return pl.pallas_call(
        kern,
        grid=(t // tm, n // tn),
        in_specs=[
            pl.BlockSpec((tm, d), lambda i, j: (i, 0)),
            pl.BlockSpec((1, N_ADA, d), lambda i, j: (i // steps_per_batch, 0, 0)),
            pl.BlockSpec((1, d), lambda i, j: (0, 0)),
            pl.BlockSpec((d, tn), lambda i, j: (0, j)),
            pl.BlockSpec((1, SB_HEAD_DIM), lambda i, j: (0, 0)),
            pl.BlockSpec((1, SB_HEAD_DIM), lambda i, j: (0, 0)),
        ],
        out_specs=pl.BlockSpec((tm, tn), lambda i, j: (i, j)),
        out_shape=jax.ShapeDtypeStruct((t, n), BF16),
        scratch_shapes=[pltpu.VMEM((tm, d), BF16)],
        compiler_params=_params(("parallel", "arbitrary")),
        name="inproj",
    )(x2, mod3, g, w, qg, kg)


S5_CHUNKS = 8
S5_CHUNK_IN = LANES
S5_CHUNK_STATE = 8 * SSM_STATE
S5_SLABS = S5_CHUNK_STATE // LANES
S5_PITCH_PAD = SUBLANES


def _s5_kernel(u_ref, br_ref, bi_ref, cr_ref, ci_ref, lamr_ref, lami_ref, d_ref, o_ref,
               bur, bui, xrs, xis, sr, si, *, tm, pitch):
    @pl.when(pl.program_id(1) == 0)
    def _():
        sr[...] = jnp.zeros_like(sr)
        si[...] = jnp.zeros_like(si)

    for c in range(S5_CHUNKS):
        uc = u_ref[:, c * S5_CHUNK_IN:(c + 1) * S5_CHUNK_IN]
        r = jnp.dot(uc, br_ref[c], preferred_element_type=F32)
        im = jnp.dot(uc, bi_ref[c], preferred_element_type=F32)
        for l in range(S5_SLABS):
            bur[l, c * pitch:c * pitch + tm, :] = r[:, l * LANES:(l + 1) * LANES]
            bui[l, c * pitch:c * pitch + tm, :] = im[:, l * LANES:(l + 1) * LANES]

    lamr = [lamr_ref[:, l * LANES:(l + 1) * LANES] for l in range(S5_SLABS)]
    lami = [lami_ref[:, l * LANES:(l + 1) * LANES] for l in range(S5_SLABS)]

    def body(t, carry):
        xr, xi = carry
        nxr, nxi = [], []
        for l in range(S5_SLABS):
            rows = pl.ds(t, S5_CHUNKS, stride=pitch)
            b_r = bur[l, rows, :]
            b_i = bui[l, rows, :]
            n_r = lamr[l] * xr[l] - lami[l] * xi[l] + b_r
            n_i = lamr[l] * xi[l] + lami[l] * xr[l] + b_i
            xrs[l, rows, :] = n_r
            xis[l, rows, :] = n_i
            nxr.append(n_r)
            nxi.append(n_i)
        return tuple(nxr), tuple(nxi)

    init = (tuple(sr[:, l * LANES:(l + 1) * LANES] for l in range(S5_SLABS)),
            tuple(si[:, l * LANES:(l + 1) * LANES] for l in range(S5_SLABS)))
    xr, xi = lax.fori_loop(0, tm, body, init, unroll=8)
    for l in range(S5_SLABS):
        sr[:, l * LANES:(l + 1) * LANES] = xr[l]
        si[:, l * LANES:(l + 1) * LANES] = xi[l]

    for c in range(S5_CHUNKS):
        state = jnp.concatenate(
            [ref[l, c * pitch:c * pitch + tm, :].astype(BF16) for ref in (xrs, xis) for l in range(S5_SLABS)],
            axis=1)
        weight = jnp.concatenate([cr_ref[c], ci_ref[c]], axis=0)
        acc = jnp.dot(state, weight, preferred_element_type=F32)
        cols = slice(c * S5_CHUNK_IN, (c + 1) * S5_CHUNK_IN)
        y = acc + d_ref[:, cols] * u_ref[:, cols].astype(F32)
        o_ref[:, cols] = _gelu(y).astype(o_ref.dtype)


def _s5_operands(lam_re, lam_im, log_dt, b_re, b_im, c_re, c_im, d_skip):
    g, p = lam_re.shape
    i_dim = b_re.shape[-1]
    dt = jnp.exp(log_dt)[:, None]
    mag = jnp.exp(lam_re * dt)
    lbr = mag * jnp.cos(lam_im * dt)
    lbi = mag * jnp.sin(lam_im * dt)
    nr, ni = lbr - 1.0, lbi
    den = lam_re * lam_re + lam_im * lam_im
    cfr = (nr * lam_re + ni * lam_im) / den
    cfi = (ni * lam_re - nr * lam_im) / den
    bbr = cfr[..., None] * b_re - cfi[..., None] * b_im
    bbi = cfr[..., None] * b_im + cfi[..., None] * b_re
    gpc = g // S5_CHUNKS
    eye = jnp.eye(gpc, dtype=F32)

    def pack_b(m):
        m = m.reshape(S5_CHUNKS, gpc, p, i_dim)
        return jnp.einsum("cgpi,gh->cgihp", m, eye).reshape(S5_CHUNKS, gpc * i_dim, gpc * p)

    def pack_c(m):
        m = m.reshape(S5_CHUNKS, gpc, i_dim, p)
        return jnp.einsum("cgip,gh->cgphi", m, eye).reshape(S5_CHUNKS, gpc * p, gpc * i_dim)

    return (pack_b(bbr).astype(BF16), pack_b(bbi).astype(BF16),
            pack_c(c_re).astype(BF16), pack_c(-c_im).astype(BF16),
            lbr.reshape(S5_CHUNKS, gpc * p), lbi.reshape(S5_CHUNKS, gpc * p),
            d_skip.reshape(1, g * i_dim))


def _s5(proj, ops, bsz, seq, width):
    br, bi, cr, ci, lamr, lami, dsk = ops
    tm = _pick(seq, 256)
    pitch = tm + S5_PITCH_PAD
    nt = seq // tm
    kern = functools.partial(_s5_kernel, tm=tm, pitch=pitch)
    full = lambda a: pl.BlockSpec(a.shape, lambda b, i: (0,) * a.ndim)
    scr = pltpu.VMEM((S5_SLABS, S5_CHUNKS * pitch, LANES), F32)
    return pl.pallas_call(
        kern,
        grid=(bsz, nt),
        in_specs=[pl.BlockSpec((tm, width), lambda b, i: (b * nt + i, 0)),
                  full(br), full(bi), full(cr), full(ci), full(lamr), full(lami), full(dsk)],
        out_specs=pl.BlockSpec((tm, width), lambda b, i: (b * nt + i, 0)),
        out_shape=jax.ShapeDtypeStruct((bsz * seq, width), BF16),
        scratch_shapes=[scr, scr, scr, scr,
                        pltpu.VMEM((S5_CHUNKS, S5_CHUNK_STATE), F32),
                        pltpu.VMEM((S5_CHUNKS, S5_CHUNK_STATE), F32)],
        compiler_params=_params(("parallel", "arbitrary")),
        name="s5",
    )(proj, br, bi, cr, ci, lamr, lami, dsk)


ATTN_GROUP = 4
ATTN_HEADS = 2
ATTN_DEAD_CARRY = 160.0


def _neg_abs(z):
    bits = lax.bitcast_convert_type(z, jnp.uint32) | jnp.uint32(0x80000000)
    return lax.bitcast_convert_type(bits, F32)


def _attn_kernel(q_ref, k_ref, v_ref, tri_ref, o_ref, acc_ref, carry_ref, *, tb):
    qi = pl.program_id(2)
    acc_ref[...] = jnp.zeros_like(acc_ref)
    carry_ref[...] = jnp.zeros_like(carry_ref)

    def group(kb_lo, n, diagonal):
        ks = pl.multiple_of(kb_lo * tb, tb)
        for hh in range(ATTN_HEADS):
            cols = slice(hh * SB_HEAD_DIM, (hh + 1) * SB_HEAD_DIM)
            q = q_ref[:, cols]
            k = k_ref[pl.ds(ks, n * tb), cols]
            v = v_ref[pl.ds(ks, n * tb), cols]
            carry = carry_ref[hh]
            ws = [None] * n
            for g in reversed(range(n)):
                z = lax.dot_general(q, k[g * tb:(g + 1) * tb, :], (((1,), (1,)), ((), ())),
                                    preferred_element_type=F32)
                sp = jnp.maximum(z, 0.0) + jnp.log(1.0 + jnp.exp2(_neg_abs(z))) * LOG2E
                masked = diagonal and g == n - 1
                if masked:
                    row = lax.broadcasted_iota(jnp.int32, (tb, tb), 0)
                    col = lax.broadcasted_iota(jnp.int32, (tb, tb), 1)
                    causal = col < row
                    sp_sum = jnp.where(causal, sp, 0.0)
                else:
                    sp_sum = sp
                cum = jnp.dot(sp_sum.astype(BF16), tri_ref[...], preferred_element_type=F32)
                w = jnp.exp2(z - sp - cum - carry)
                if masked:
                    w = jnp.where(causal, w, 0.0)
                ws[g] = w.astype(BF16)
                carry = carry + jnp.sum(sp_sum, axis=1, keepdims=True)
            w_all = ws[0] if n == 1 else jnp.concatenate(ws, axis=1)
            acc_ref[:, cols] += jnp.dot(w_all, v, preferred_element_type=F32)
            carry_ref[hh] = carry

    def live():
        return (jnp.min(carry_ref[...]) < ATTN_DEAD_CARRY).astype(jnp.int32)

    group(qi, 1, True)

    @pl.when(jnp.logical_and(qi > 0, live() > 0))
    def _():
        group(qi - 1, 1, False)

    rest = jnp.maximum(qi - 1, 0)
    done = jnp.int32(0)
    size = 1
    while size < ATTN_GROUP:
        bit = (rest & size) != 0

        @pl.when(jnp.logical_and(bit, live() > 0))
        def _(done=done, size=size):
            group(rest - done - size, size, False)

        done = done + jnp.where(bit, size, 0)
        size *= 2

    n_full = rest // ATTN_GROUP

    def more(state):
        it, alive = state
        return jnp.logical_and(it < n_full, alive > 0)

    def multi(state):
        it, _ = state
        group(rest - done - (it + 1) * ATTN_GROUP, ATTN_GROUP, False)
        return it + 1, live()

    lax.while_loop(more, multi, (jnp.int32(0), live()))
    o_ref[...] = acc_ref[...].astype(o_ref.dtype)


def _attn(proj, bsz, seq, q_col, k_col, v_col):
    tb = _pick(seq, 256)
    nq = seq // tb
    r = lax.broadcasted_iota(jnp.int32, (tb, tb), 0)
    c = lax.broadcasted_iota(jnp.int32, (tb, tb), 1)
    tri = jnp.where(r > c, 1.0, 0.0).astype(BF16)
    kern = functools.partial(_attn_kernel, tb=tb)
    hw = ATTN_HEADS * SB_HEAD_DIM
    assert q_col % ATTN_HEADS == 0 and SB_HEADS % ATTN_HEADS == 0
    qc, kc, vc = q_col // ATTN_HEADS, k_col // ATTN_HEADS, v_col // ATTN_HEADS
    return pl.pallas_call(
        kern,
        grid=(bsz, SB_HEADS // ATTN_HEADS, nq),
        in_specs=[
            pl.BlockSpec((tb, hw), lambda b, h, i: (b * nq + i, qc + h)),
            pl.BlockSpec((seq, hw), lambda b, h, i: (b, kc + h)),
            pl.BlockSpec((seq, hw), lambda b, h, i: (b, vc + h)),
            pl.BlockSpec((tb, tb), lambda b, h, i: (0, 0)),
        ],
        out_specs=pl.BlockSpec((tb, hw), lambda b, h, i: (b * nq + i, h)),
        out_shape=jax.ShapeDtypeStruct((bsz * seq, SB_HEADS * SB_HEAD_DIM), BF16),
        scratch_shapes=[pltpu.VMEM((tb, hw), F32), pltpu.VMEM((ATTN_HEADS, tb, 1), F32)],
        compiler_params=_params(("parallel", "parallel", "arbitrary")),
        name="attn",
    )(proj, proj, proj, tri)


def _mix_kernel(y_ref, att_ref, gs_ref, ga_ref, x_ref, mod_ref, wglu_ref, wup_ref, wout_ref, g2_ref,
                x1_ref, h2_ref, h2t_ref, *, d):
    yg = jnp.dot(y_ref[...], wglu_ref[...], preferred_element_type=F32)
    ssm = yg[:, :d] * jax.nn.sigmoid(yg[:, d:])
    up = jnp.dot(att_ref[...], wup_ref[...], preferred_element_type=F32)
    merged = (jax.nn.sigmoid(gs_ref[...].astype(F32)) * ssm
              + jax.nn.sigmoid(ga_ref[...].astype(F32)) * up)
    o = jnp.dot(merged.astype(BF16), wout_ref[...], preferred_element_type=F32)
    x1 = x_ref[...] + mod_ref[0, 2:3, :] * o
    x1_ref[...] = x1
    ms = jnp.mean(x1 * x1, axis=-1, keepdims=True)
    h2 = x1 * lax.rsqrt(ms + EPS) * g2_ref[...]
    h2 = h2 * (1.0 + mod_ref[0, 4:5, :]) + mod_ref[0, 3:4, :]
    h2_ref[...] = h2.astype(BF16)
    h2t_ref[...] = h2.T.astype(BF16)


def _mix(ygelu, att, proj, x2, mod3, wglu, wup, wout, g2, seq, gs_col, ga_col):
    t, d = x2.shape
    w = ygelu.shape[1]
    tm = _pick(seq, 256)
    spb = seq // tm
    kern = functools.partial(_mix_kernel, d=d)
    const = lambda a: pl.BlockSpec(a.shape, lambda i: (0,) * a.ndim)
    return pl.pallas_call(
        kern,
        grid=(t // tm,),
        in_specs=[
            pl.BlockSpec((tm, w), lambda i: (i, 0)),
            pl.BlockSpec((tm, w), lambda i: (i, 0)),
            pl.BlockSpec((tm, d), lambda i: (i, gs_col)),
            pl.BlockSpec((tm, d), lambda i: (i, ga_col)),
            pl.BlockSpec((tm, d), lambda i: (i, 0)),
            pl.BlockSpec((1, N_ADA, d), lambda i: (i // spb, 0, 0)),
            const(wglu), const(wup), const(wout), const(g2),
        ],
        out_specs=[
            pl.BlockSpec((tm, d), lambda i: (i, 0)),
            pl.BlockSpec((tm, d), lambda i: (i, 0)),
            pl.BlockSpec((d, tm), lambda i: (0, i)),
        ],
        out_shape=[jax.ShapeDtypeStruct((t, d), F32),
                   jax.ShapeDtypeStruct((t, d), BF16),
                   jax.ShapeDtypeStruct((d, t), BF16)],
        compiler_params=_params(("parallel",)),
        name="mix",
    )(ygelu, att, proj, proj, x2, mod3, wglu, wup, wout, g2)


PEER_RANKS = PEER_TOPK + 1
PEER_RANK_ROWS = 24


def _candidate_tiles(v1_scr, v2_scr, tm):
    tiles = []
    row8 = lax.broadcasted_iota(jnp.int32, (SUBLANES, tm), 0)
    for b in range(PEER_RANKS):
        n_a = PEER_RANKS // (b + 1)
        v2b = v2_scr[b:b + 1, :]
        for base in range(0, n_a, SUBLANES):
            t = v1_scr[base:base + SUBLANES, :] + v2b
            if n_a - base < SUBLANES:
                t = jnp.where(row8 < (n_a - base), t, NEG_INF)
            tiles.append(t)
    return tiles


def _sort_network(n):
    pairs = []
    p = 1
    while p < n:
        k = p
        while k >= 1:
            for j in range(k % p, n - k, 2 * k):
                for i in range(min(k, n - j - k)):
                    if (i + j) // (2 * p) == (i + j + k) // (2 * p):
                        pairs.append((i + j, i + j + k))
            k //= 2
        p *= 2
    return pairs


def _top_values(s, v_scr):
    n_tiles = s.shape[0] // SUBLANES
    rows = [s[k * SUBLANES:(k + 1) * SUBLANES, :] for k in range(n_tiles)]
    for i, j in _sort_network(n_tiles):
        rows[i], rows[j] = jnp.maximum(rows[i], rows[j]), jnp.minimum(rows[i], rows[j])
    for a in range(PEER_RANKS):
        m = jnp.max(rows[0], axis=0, keepdims=True)
        v_scr[a:a + 1, :] = m
        hit = rows[0] >= m
        for k in range(min(n_tiles, PEER_RANKS - 1 - a)):
            below = rows[k + 1] if k + 1 < n_tiles else NEG_INF
            rows[k] = jnp.where(hit, below, rows[k])


def _route_kernel(h2_ref, wq_ref, k1_ref, k2_ref, c1_ref, w1_ref, e2_ref,
                  s_scr, v1_scr, v2_scr, *, tm):
    q = jnp.dot(h2_ref[...], wq_ref[...], preferred_element_type=F32).astype(k1_ref.dtype)
    v1_scr[...] = jnp.full(v1_scr.shape, NEG_INF, F32)
    v2_scr[...] = jnp.full(v2_scr.shape, NEG_INF, F32)
    for h in range(PEER_HEADS):
        for half, (k_ref, v_scr) in enumerate(((k1_ref, v1_scr), (k2_ref, v2_scr))):
            col = (2 * h + half) * PEER_HALF
            qh = q[:, col:col + PEER_HALF]
            s = lax.dot_general(k_ref[...], qh, (((1,), (1,)), ((), ())), preferred_element_type=F32)
            s_scr[half] = s
            _top_values(s, v_scr)
        tiles = _candidate_tiles(v1_scr, v2_scr, tm)
        work = list(tiles)
        ranked = []
        for a in range(PEER_RANKS):
            m = work[0]
            for t in work[1:]:
                m = jnp.maximum(m, t)
            m = jnp.max(m, axis=0, keepdims=True)
            ranked.append(m)
            work = [jnp.where(t >= m, NEG_INF, t) for t in work]
        top = ranked[0]
        thr = 0.5 * (ranked[PEER_TOPK - 1] + ranked[PEER_TOPK])
        zsum = jnp.zeros((SUBLANES, tm), F32)
        for t in tiles:
            zsum = zsum + jnp.where(t >= thr, jnp.exp(t - top), 0.0)
        inv_z = 1.0 / jnp.sum(zsum, axis=0, keepdims=True)
        s1 = s_scr[0]
        s2 = s_scr[1]
        top2 = v2_scr[0:1, :]
        c1_ref[h] = jnp.exp(thr - s1 - top2)
        w1_ref[h] = jnp.exp(s1 - v1_scr[0:1, :]) * inv_z
        e2_ref[h] = jnp.exp(s2 - top2)


def _route(h2, wq, k1, k2):
    t, d = h2.shape
    tm = _pick(t, 256)
    kern = functools.partial(_route_kernel, tm=tm)
    const = lambda a: pl.BlockSpec(a.shape, lambda i: (0,) * a.ndim)
    out_spec = pl.BlockSpec((PEER_HEADS, PEER_N_KEYS, tm), lambda i: (0, 0, i))
    out_sds = jax.ShapeDtypeStruct((PEER_HEADS, PEER_N_KEYS, t), F32)
    return pl.pallas_call(
        kern,
        grid=(t // tm,),
        in_specs=[pl.BlockSpec((tm, d), lambda i: (i, 0)), const(wq), const(k1), const(k2)],
        out_specs=[out_spec] * 3,
        out_shape=[out_sds] * 3,
        scratch_shapes=[pltpu.VMEM((2, PEER_N_KEYS, tm), F32),
                        pltpu.VMEM((PEER_RANK_ROWS, tm), F32),
                        pltpu.VMEM((PEER_RANK_ROWS, tm), F32)],
        compiler_params=_params(("parallel",)),
        name="route",
    )(h2, wq, k1, k2)


EXPERT_TILE = 1024
GATE_ROWS = 32


def _expert_gated(i_first, n_i, row0, act_r, c1_ref, w1_ref, e2_ref, tm):
    ps = []
    for ii in range(n_i):
        i = i_first + ii
        rows = slice(row0 + ii * PEER_N_KEYS, row0 + (ii + 1) * PEER_N_KEYS)
        c_rows = [c1_ref[h, pl.ds(i, 1), :] for h in range(PEER_HEADS)]
        w_rows = [w1_ref[h, pl.ds(i, 1), :] for h in range(PEER_HEADS)]
        chunks = []
        for r0 in range(0, PEER_N_KEYS, GATE_ROWS):
            gate = None
            for h in range(PEER_HEADS):
                e2 = e2_ref[h, r0:r0 + GATE_ROWS, :]
                term = jnp.where(e2 >= c_rows[h], e2 * w_rows[h], 0.0)
                gate = term if gate is None else gate + term
            act = act_r[rows.start + r0:rows.start + r0 + GATE_ROWS, :]
            chunks.append((gate * _gelu(act)).astype(BF16))
        ps.append(jnp.concatenate(chunks, axis=0))
    return ps[0] if len(ps) == 1 else jnp.concatenate(ps, axis=0)


def _experts_kernel(u_ref, h2t_ref, vt_ref, c1_ref, w1_ref, e2_ref, x1_ref, mod_ref, o_ref,
                    acc_ref, act0, act1, *, te, tm, ne):
    e = pl.program_id(1)
    n_i = te // PEER_N_KEYS
    slots = (act0, act1)

    def step(act_w, act_r, i0):
        if act_w is not None:
            act_w[...] = jnp.dot(u_ref[...], h2t_ref[...], preferred_element_type=F32)
        if act_r is not None:
            p = _expert_gated(i0, n_i, 0, act_r, c1_ref, w1_ref, e2_ref, tm)
            acc_ref[...] += jnp.dot(vt_ref[0], p, preferred_element_type=F32)

    @pl.when(e == 0)
    def _():
        acc_ref[...] = jnp.zeros_like(acc_ref)
        step(slots[0], None, None)

    for parity in range(2):
        @pl.when(jnp.logical_and(jnp.logical_and(e > 0, e < ne), lax.rem(e, 2) == parity))
        def _(parity=parity):
            step(slots[parity], slots[1 - parity], (e - 1) * n_i)

    @pl.when(e == ne)
    def _():
        step(None, slots[(ne - 1) % 2], (ne - 1) * n_i)
        o_ref[...] = x1_ref[...] + mod_ref[0, 5:6, :] * acc_ref[...].T


def _experts(u_tab, h2t, v_tab, route_out, x1, mod3, seq):
    n_e, d = u_tab.shape
    t = h2t.shape[1]
    tm = _pick(seq, 512)
    te = _pick(n_e, EXPERT_TILE)
    ne = n_e // te
    spb = seq // tm
    vt_tiles = v_tab.reshape(ne, te, d).transpose(0, 2, 1).astype(BF16)
    kern = functools.partial(_experts_kernel, te=te, tm=tm, ne=ne)
    rspec = pl.BlockSpec((PEER_HEADS, PEER_N_KEYS, tm), lambda i, e: (0, 0, i))
    return pl.pallas_call(
        kern,
        grid=(t // tm, ne + 1),
        in_specs=[
            pl.BlockSpec((te, d), lambda i, e: (jnp.minimum(e, ne - 1), 0)),
            pl.BlockSpec((d, tm), lambda i, e: (0, i)),
            pl.BlockSpec((1, d, te), lambda i, e: (jnp.maximum(e - 1, 0), 0, 0)),
            rspec, rspec, rspec,
            pl.BlockSpec((tm, d), lambda i, e: (i, 0), pipeline_mode=pl.Buffered(1)),
            pl.BlockSpec((1, N_ADA, d), lambda i, e: (i // spb, 0, 0)),
        ],
        out_specs=pl.BlockSpec((tm, d), lambda i, e: (i, 0), pipeline_mode=pl.Buffered(1)),
        out_shape=jax.ShapeDtypeStruct((t, d), F32),
        scratch_shapes=[pltpu.VMEM((d, tm), F32), pltpu.VMEM((te, tm), F32), pltpu.VMEM((te, tm), F32)],
        compiler_params=_params(("parallel", "arbitrary")),
        name="experts",
    )(u_tab, h2t, vt_tiles, *route_out, x1, mod3)


def kernel(x, c, w_ada, b_ada, norm1_g, w_in, lam_re, lam_im, log_dt, ssm_b_re, ssm_b_im, ssm_c_re, ssm_c_im, ssm_d, w_glu, q_norm_g, k_norm_g, w_att_up, w_out, norm2_g, peer_wq, peer_k1, peer_k2, peer_u, peer_v):
    bsz, seq, d = x.shape
    depth = w_ada.shape[0]
    ssm_width = ssm_d.shape[1] * ssm_d.shape[2]
    sb_width = SB_HEADS * SB_HEAD_DIM
    assert ssm_width == sb_width and d % sb_width == 0
    assert ssm_d.shape[1] % S5_CHUNKS == 0 and ssm_d.shape[2] == SSM_GROUP and lam_re.shape[2] == SSM_STATE
    q_col = sb_width // SB_HEAD_DIM
    k_col = 2 * q_col
    v_col = 3 * q_col
    gate_base = 4 * sb_width
    assert gate_base % d == 0
    gs_col = gate_base // d
    ga_col = gs_col + 1

    x2 = x.reshape(bsz * seq, d)
    c_pad = jnp.zeros((SUBLANES, d), F32).at[:bsz].set(c)
    for l in range(depth):
        mod = _ada(c_pad, w_ada[l], b_ada[l].reshape(1, -1))
        mod3 = mod.reshape(SUBLANES, N_ADA, d)
        proj = _inproj(x2, mod3, norm1_g[l].reshape(1, d), w_in[l].astype(BF16),
                       q_norm_g[l].reshape(1, -1), k_norm_g[l].reshape(1, -1), seq, sb_width)
        s5_ops = _s5_operands(lam_re[l], lam_im[l], log_dt[l], ssm_b_re[l], ssm_b_im[l],
                              ssm_c_re[l], ssm_c_im[l], ssm_d[l])
        ygelu = _s5(proj, s5_ops, bsz, seq, ssm_width)
        att = _attn(proj, bsz, seq, q_col, k_col, v_col)
        x1, h2, h2t = _mix(ygelu, att, proj, x2, mod3, w_glu[l].astype(BF16), w_att_up[l].astype(BF16),
                           w_out[l].astype(BF16), norm2_g[l].reshape(1, d), seq, gs_col, ga_col)
        route_out = _route(h2, peer_wq[l].astype(BF16), peer_k1[l].astype(BF16), peer_k2[l].astype(BF16))
        x2 = _experts(peer_u[l].astype(BF16), h2t, peer_v[l], route_out, x1, mod3, seq)
    return x2.reshape(bsz, seq, d)
```

```python
import functools
import math

import jax
import jax.numpy as jnp
from jax import lax
from jax.experimental import pallas as pl
from jax.experimental.pallas import tpu as pltpu

F32 = jnp.float32
BF16 = jnp.bfloat16

LANES = 128
SUBLANES = 8
VMEM_LIMIT = 56 * 1024 * 1024

SSM_GROUP = 16
SSM_STATE = 64
SB_HEADS = 8
SB_HEAD_DIM = 128
PEER_HEADS = 8
PEER_N_KEYS = 128
PEER_TOPK = 16
PEER_HALF = 128
N_ADA = 6
EPS = 1e-6
NEG_INF = float("-inf")
LOG2E = math.log2(math.e)


def _params(sem):
    return pltpu.CompilerParams(dimension_semantics=sem, vmem_limit_bytes=VMEM_LIMIT)


GELU_C = -2.0 * math.sqrt(2.0 / math.pi) * LOG2E


def _gelu(x):
    return x / (1.0 + jnp.exp2(x * (GELU_C + (GELU_C * 0.044715) * (x * x))))


def _pick(n, pref):
    t = min(n, pref)
    while n % t:
        t //= 2
    return t


def _ada_kernel(c_ref, w_ref, b_ref, o_ref):
    c = c_ref[...]
    c = c * jax.nn.sigmoid(c)
    o_ref[...] = jnp.dot(c, w_ref[...], preferred_element_type=F32) + b_ref[...]


def _ada(c_pad, w, b):
    rows, d = c_pad.shape
    n = w.shape[1]
    tn = _pick(n, 1024)
    return pl.pallas_call(
        _ada_kernel,
        grid=(n // tn,),
        in_specs=[
            pl.BlockSpec((rows, d), lambda j: (0, 0)),
            pl.BlockSpec((d, tn), lambda j: (0, j)),
            pl.BlockSpec((1, tn), lambda j: (0, j)),
        ],
        out_specs=pl.BlockSpec((rows, tn), lambda j: (0, j)),
        out_shape=jax.ShapeDtypeStruct((rows, n), F32),
        compiler_params=_params(("arbitrary",)),
        name="ada",
    )(c_pad, w, b)


def _head_rms(acc, g_ref, o_ref, scale):
    for h in range(acc.shape[1] // SB_HEAD_DIM):
        blk = acc[:, h * SB_HEAD_DIM:(h + 1) * SB_HEAD_DIM]
        ms = jnp.mean(blk * blk, axis=-1, keepdims=True)
        y = blk * lax.rsqrt(ms + EPS) * g_ref[...]
        if scale != 1.0:
            y = y * scale
        o_ref[:, h * SB_HEAD_DIM:(h + 1) * SB_HEAD_DIM] = y.astype(o_ref.dtype)


def _inproj_kernel(x_ref, mod_ref, g_ref, w_ref, qg_ref, kg_ref, o_ref, h_scr, *, q_blk, k_blk):
    j = pl.program_id(1)

    @pl.when(j == 0)
    def _():
        x = x_ref[...]
        ms = jnp.mean(x * x, axis=-1, keepdims=True)
        y = x * lax.rsqrt(ms + EPS) * g_ref[...]
        h = y * (1.0 + mod_ref[0, 1:2, :]) + mod_ref[0, 0:1, :]
        h_scr[...] = h.astype(BF16)

    acc = jnp.dot(h_scr[...], w_ref[...], preferred_element_type=F32)

    @pl.when(j == q_blk)
    def _():
        _head_rms(acc, qg_ref, o_ref, SB_HEAD_DIM ** -0.5 * LOG2E)

    @pl.when(j == k_blk)
    def _():
        _head_rms(acc, kg_ref, o_ref, 1.0)

    @pl.when(jnp.logical_and(j != q_blk, j != k_blk))
    def _():
        o_ref[...] = acc.astype(o_ref.dtype)


def _inproj(x2, mod3, g, w, qg, kg, seq, sb_width):
    t, d = x2.shape
    n = w.shape[1]
    tm = _pick(seq, 1024)
    tn = sb_width
    steps_per_batch = seq // tm
    kern = functools.partial(_inproj_kernel, q_blk=1, k_blk=2)
    return pl.pallas_call(
        kern,
        grid=(t // tm, n // tn),
        in_specs=[
            pl.BlockSpec((tm, d), lambda i, j: (i, 0)),
            pl.BlockSpec((1, N_ADA, d), lambda i, j: (i // steps_per_batch, 0, 0)),
            pl.BlockSpec((1, d), lambda i, j: (0, 0)),
            pl.BlockSpec((d, tn), lambda i, j: (0, j)),
            pl.BlockSpec((1, SB_HEAD_DIM), lambda i, j: (0, 0)),
            pl.BlockSpec((1, SB_HEAD_DIM), lambda i, j: (0, 0)),
        ],
        out_specs=pl.BlockSpec((tm, tn), lambda i, j: (i, j)),
        out_shape=jax.ShapeDtypeStruct((t, n), BF16),
        scratch_shapes=[pltpu.VMEM((tm, d), BF16)],
        compiler_params=_params(("parallel", "arbitrary")),
        name="inproj",
    )(x2, mod3, g, w, qg, kg)


S5_CHUNKS = 8
S5_CHUNK_IN = LANES
S5_CHUNK_STATE = 8 * SSM_STATE
S5_SLABS = S5_CHUNK_STATE // LANES
S5_PITCH_PAD = SUBLANES


def _s5_kernel(u_ref, br_ref, bi_ref, cr_ref, ci_ref, lamr_ref, lami_ref, d_ref, o_ref,
               bur, bui, xrs, xis, sr, si, *, tm, pitch):
    @pl.when(pl.program_id(1) == 0)
    def _():
        sr[...] = jnp.zeros_like(sr)
        si[...] = jnp.zeros_like(si)

    for c in range(S5_CHUNKS):
        uc = u_ref[:, c * S5_CHUNK_IN:(c + 1) * S5_CHUNK_IN]
        r = jnp.dot(uc, br_ref[c], preferred_element_type=F32)
        im = jnp.dot(uc, bi_ref[c], preferred_element_type=F32)
        for l in range(S5_SLABS):
            bur[l, c * pitch:c * pitch + tm, :] = r[:, l * LANES:(l + 1) * LANES]
            bui[l, c * pitch:c * pitch + tm, :] = im[:, l * LANES:(l + 1) * LANES]

    lamr = [lamr_ref[:, l * LANES:(l + 1) * LANES] for l in range(S5_SLABS)]
    lami = [lami_ref[:, l * LANES:(l + 1) * LANES] for l in range(S5_SLABS)]

    def body(t, carry):
        xr, xi = carry
        nxr, nxi = [], []
        for l in range(S5_SLABS):
            rows = pl.ds(t, S5_CHUNKS, stride=pitch)
            b_r = bur[l, rows, :]
            b_i = bui[l, rows, :]
            n_r = lamr[l] * xr[l] - lami[l] * xi[l] + b_r
            n_i = lamr[l] * xi[l] + lami[l] * xr[l] + b_i
            xrs[l, rows, :] = n_r
            xis[l, rows, :] = n_i
            nxr.append(n_r)
            nxi.append(n_i)
        return tuple(nxr), tuple(nxi)

    init = (tuple(sr[:, l * LANES:(l + 1) * LANES] for l in range(S5_SLABS)),
            tuple(si[:, l * LANES:(l + 1) * LANES] for l in range(S5_SLABS)))
    xr, xi = lax.fori_loop(0, tm, body, init, unroll=8)
    for l in range(S5_SLABS):
        sr[:, l * LANES:(l + 1) * LANES] = xr[l]
        si[:, l * LANES:(l + 1) * LANES] = xi[l]

    for c in range(S5_CHUNKS):
        state = jnp.concatenate(
            [ref[l, c * pitch:c * pitch + tm, :].astype(BF16) for ref in (xrs, xis) for l in range(S5_SLABS)],
            axis=1)
        weight = jnp.concatenate([cr_ref[c], ci_ref[c]], axis=0)
        acc = jnp.dot(state, weight, preferred_element_type=F32)
        cols = slice(c * S5_CHUNK_IN, (c + 1) * S5_CHUNK_IN)
        y = acc + d_ref[:, cols] * u_ref[:, cols].astype(F32)
        o_ref[:, cols] = _gelu(y).astype(o_ref.dtype)


def _s5_operands(lam_re, lam_im, log_dt, b_re, b_im, c_re, c_im, d_skip):
    g, p = lam_re.shape
    i_dim = b_re.shape[-1]
    dt = jnp.exp(log_dt)[:, None]
    mag = jnp.exp(lam_re * dt)
    lbr = mag * jnp.cos(lam_im * dt)
    lbi = mag * jnp.sin(lam_im * dt)
    nr, ni = lbr - 1.0, lbi
    den = lam_re * lam_re + lam_im * lam_im
    cfr = (nr * lam_re + ni * lam_im) / den
    cfi = (ni * lam_re - nr * lam_im) / den
    bbr = cfr[..., None] * b_re - cfi[..., None] * b_im
    bbi = cfr[..., None] * b_im + cfi[..., None] * b_re
    gpc = g // S5_CHUNKS
    eye = jnp.eye(gpc, dtype=F32)

    def pack_b(m):
        m = m.reshape(S5_CHUNKS, gpc, p, i_dim)
        return jnp.einsum("cgpi,gh->cgihp", m, eye).reshape(S5_CHUNKS, gpc * i_dim, gpc * p)

    def pack_c(m):
        m = m.reshape(S5_CHUNKS, gpc, i_dim, p)
        return jnp.einsum("cgip,gh->cgphi", m, eye).reshape(S5_CHUNKS, gpc * p, gpc * i_dim)

    return (pack_b(bbr).astype(BF16), pack_b(bbi).astype(BF16),
            pack_c(c_re).astype(BF16), pack_c(-c_im).astype(BF16),
            lbr.reshape(S5_CHUNKS, gpc * p), lbi.reshape(S5_CHUNKS, gpc * p),
            d_skip.reshape(1, g * i_dim))


def _s5(proj, ops, bsz, seq, width):
    br, bi, cr, ci, lamr, lami, dsk = ops
    tm = _pick(seq, 256)
    pitch = tm + S5_PITCH_PAD
    nt = seq // tm
    kern = functools.partial(_s5_kernel, tm=tm, pitch=pitch)
    full = lambda a: pl.BlockSpec(a.shape, lambda b, i: (0,) * a.ndim)
    scr = pltpu.VMEM((S5_SLABS, S5_CHUNKS * pitch, LANES), F32)
    return pl.pallas_call(
        kern,
        grid=(bsz, nt),
        in_specs=[pl.BlockSpec((tm, width), lambda b, i: (b * nt + i, 0)),
                  full(br), full(bi), full(cr), full(ci), full(lamr), full(lami), full(dsk)],
        out_specs=pl.BlockSpec((tm, width), lambda b, i: (b * nt + i, 0)),
        out_shape=jax.ShapeDtypeStruct((bsz * seq, width), BF16),
        scratch_shapes=[scr, scr, scr, scr,
                        pltpu.VMEM((S5_CHUNKS, S5_CHUNK_STATE), F32),
                        pltpu.VMEM((S5_CHUNKS, S5_CHUNK_STATE), F32)],
        compiler_params=_params(("parallel", "arbitrary")),
        name="s5",
    )(proj, br, bi, cr, ci, lamr, lami, dsk)


ATTN_GROUP = 4
ATTN_HEADS = 2
ATTN_DEAD_CARRY = 160.0


def _neg_abs(z):
    bits = lax.bitcast_convert_type(z, jnp.uint32) | jnp.uint32(0x80000000)
    return lax.bitcast_convert_type(bits, F32)


def _attn_kernel(q_ref, k_ref, v_ref, tri_ref, o_ref, acc_ref, carry_ref, *, tb):
    qi = pl.program_id(2)
    acc_ref[...] = jnp.zeros_like(acc_ref)
    carry_ref[...] = jnp.zeros_like(carry_ref)

    def group(kb_lo, n, diagonal):
        ks = pl.multiple_of(kb_lo * tb, tb)
        for hh in range(ATTN_HEADS):
            cols = slice(hh * SB_HEAD_DIM, (hh + 1) * SB_HEAD_DIM)
            q = q_ref[:, cols]
            k = k_ref[pl.ds(ks, n * tb), cols]
            v = v_ref[pl.ds(ks, n * tb), cols]
            carry = carry_ref[hh]
            ws = [None] * n
            for g in reversed(range(n)):
                z = lax.dot_general(q, k[g * tb:(g + 1) * tb, :], (((1,), (1,)), ((), ())),
                                    preferred_element_type=F32)
                sp = jnp.maximum(z, 0.0) + jnp.log(1.0 + jnp.exp2(_neg_abs(z))) * LOG2E
                masked = diagonal and g == n - 1
                if masked:
                    row = lax.broadcasted_iota(jnp.int32, (tb, tb), 0)
                    col = lax.broadcasted_iota(jnp.int32, (tb, tb), 1)
                    causal = col < row
                    sp_sum = jnp.where(causal, sp, 0.0)
                else:
                    sp_sum = sp
                cum = jnp.dot(sp_sum.astype(BF16), tri_ref[...], preferred_element_type=F32)
                w = jnp.exp2(z - sp - cum - carry)
                if masked:
                    w = jnp.where(causal, w, 0.0)
                ws[g] = w.astype(BF16)
                carry = carry + jnp.sum(sp_sum, axis=1, keepdims=True)
            w_all = ws[0] if n == 1 else jnp.concatenate(ws, axis=1)
            acc_ref[:, cols] += jnp.dot(w_all, v, preferred_element_type=F32)
            carry_ref[hh] = carry

    def live():
        return (jnp.min(carry_ref[...]) < ATTN_DEAD_CARRY).astype(jnp.int32)

    group(qi, 1, True)

    @pl.when(jnp.logical_and(qi > 0, live() > 0))
    def _():
        group(qi - 1, 1, False)

    @pl.when(jnp.logical_and(qi > 1, live() > 0))
    def _():
        rest = qi - 1
        done = jnp.int32(0)
        size = 1
        while size < ATTN_GROUP:
            bit = (rest & size) != 0

            @pl.when(jnp.logical_and(bit, live() > 0))
            def _(done=done, size=size):
                group(rest - done - size, size, False)

            done = done + jnp.where(bit, size, 0)
            size *= 2

        n_full = rest // ATTN_GROUP

        def more(state):
            it, alive = state
            return jnp.logical_and(it < n_full, alive > 0)

        def multi(state):
            it, _ = state
            group(rest - done - (it + 1) * ATTN_GROUP, ATTN_GROUP, False)
            return it + 1, live()

        lax.while_loop(more, multi, (jnp.int32(0), live()))

    o_ref[...] = acc_ref[...].astype(o_ref.dtype)


def _attn(proj, bsz, seq, q_col, k_col, v_col):
    tb = _pick(seq, 256)
    nq = seq // tb
    r = lax.broadcasted_iota(jnp.int32, (tb, tb), 0)
    c = lax.broadcasted_iota(jnp.int32, (tb, tb), 1)
    tri = jnp.where(r > c, 1.0, 0.0).astype(BF16)
    kern = functools.partial(_attn_kernel, tb=tb)
    hw = ATTN_HEADS * SB_HEAD_DIM
    assert q_col % ATTN_HEADS == 0 and SB_HEADS % ATTN_HEADS == 0
    qc, kc, vc = q_col // ATTN_HEADS, k_col // ATTN_HEADS, v_col // ATTN_HEADS
    return pl.pallas_call(
        kern,
        grid=(bsz, SB_HEADS // ATTN_HEADS, nq),
        in_specs=[
            pl.BlockSpec((tb, hw), lambda b, h, i: (b * nq + i, qc + h)),
            pl.BlockSpec((seq, hw), lambda b, h, i: (b, kc + h)),
            pl.BlockSpec((seq, hw), lambda b, h, i: (b, vc + h)),
            pl.BlockSpec((tb, tb), lambda b, h, i: (0, 0)),
        ],
        out_specs=pl.BlockSpec((tb, hw), lambda b, h, i: (b * nq + i, h)),
        out_shape=jax.ShapeDtypeStruct((bsz * seq, SB_HEADS * SB_HEAD_DIM), BF16),
        scratch_shapes=[pltpu.VMEM((tb, hw), F32), pltpu.VMEM((ATTN_HEADS, tb, 1), F32)],
        compiler_params=_params(("parallel", "parallel", "arbitrary")),
        name="attn",
    )(proj, proj, proj, tri)


def _mix_kernel(y_ref, att_ref, gs_ref, ga_ref, x_ref, mod_ref, wglu_ref, wup_ref, wout_ref, g2_ref,
                x1_ref, h2_ref, h2t_ref, *, d):
    yg = jnp.dot(y_ref[...], wglu_ref[...], preferred_element_type=F32)
    ssm = yg[:, :d] * jax.nn.sigmoid(yg[:, d:])
    up = jnp.dot(att_ref[...], wup_ref[...], preferred_element_type=F32)
    merged = (jax.nn.sigmoid(gs_ref[...].astype(F32)) * ssm
              + jax.nn.sigmoid(ga_ref[...].astype(F32)) * up)
    o = jnp.dot(merged.astype(BF16), wout_ref[...], preferred_element_type=F32)
    x1 = x_ref[...] + mod_ref[0, 2:3, :] * o
    x1_ref[...] = x1
    ms = jnp.mean(x1 * x1, axis=-1, keepdims=True)
    h2 = x1 * lax.rsqrt(ms + EPS) * g2_ref[...]
    h2 = h2 * (1.0 + mod_ref[0, 4:5, :]) + mod_ref[0, 3:4, :]
    h2_ref[...] = h2.astype(BF16)
    h2t_ref[...] = h2.T.astype(BF16)


def _mix(ygelu, att, proj, x2, mod3, wglu, wup, wout, g2, seq, gs_col, ga_col):
    t, d = x2.shape
    w = ygelu.shape[1]
    tm = _pick(seq, 256)
    spb = seq // tm
    kern = functools.partial(_mix_kernel, d=d)
    const = lambda a: pl.BlockSpec(a.shape, lambda i: (0,) * a.ndim)
    return pl.pallas_call(
        kern,
        grid=(t // tm,),
        in_specs=[
            pl.BlockSpec((tm, w), lambda i: (i, 0)),
            pl.BlockSpec((tm, w), lambda i: (i, 0)),
            pl.BlockSpec((tm, d), lambda i: (i, gs_col)),
            pl.BlockSpec((tm, d), lambda i: (i, ga_col)),
            pl.BlockSpec((tm, d), lambda i: (i, 0)),
            pl.BlockSpec((1, N_ADA, d), lambda i: (i // spb, 0, 0)),
            const(wglu), const(wup), const(wout), const(g2),
        ],
        out_specs=[
            pl.BlockSpec((tm, d), lambda i: (i, 0)),
            pl.BlockSpec((tm, d), lambda i: (i, 0)),
            pl.BlockSpec((d, tm), lambda i: (0, i)),
        ],
        out_shape=[jax.ShapeDtypeStruct((t, d), F32),
                   jax.ShapeDtypeStruct((t, d), BF16),
                   jax.ShapeDtypeStruct((d, t), BF16)],
        compiler_params=_params(("parallel",)),
        name="mix",
    )(ygelu, att, proj, proj, x2, mod3, wglu, wup, wout, g2)


PEER_RANKS = PEER_TOPK + 1
PEER_RANK_ROWS = 24


def _candidate_tiles(v1_scr, v2_scr, tm):
    tiles = []
    row8 = lax.broadcasted_iota(jnp.int32, (SUBLANES, tm), 0)
    for b in range(PEER_RANKS):
        n_a = PEER_RANKS // (b + 1)
        v2b = v2_scr[b:b + 1, :]
        for base in range(0, n_a, SUBLANES):
            t = v1_scr[base:base + SUBLANES, :] + v2b
            if n_a - base < SUBLANES:
                t = jnp.where(row8 < (n_a - base), t, NEG_INF)
            tiles.append(t)
    return tiles


def _sort_network(n):
    pairs = []
    p = 1
    while p < n:
        k = p
        while k >= 1:
            for j in range(k % p, n - k, 2 * k):
                for i in range(min(k, n - j - k)):
                    if (i + j) // (2 * p) == (i + j + k) // (2 * p):
                        pairs.append((i + j, i + j + k))
            k //= 2
        p *= 2
    return pairs


def _top_values(s, v_scr):
    n_tiles = s.shape[0] // SUBLANES
    rows = [s[k * SUBLANES:(k + 1) * SUBLANES, :] for k in range(n_tiles)]
    for i, j in _sort_network(n_tiles):
        rows[i], rows[j] = jnp.maximum(rows[i], rows[j]), jnp.minimum(rows[i], rows[j])
    for a in range(PEER_RANKS):
        m = jnp.max(rows[0], axis=0, keepdims=True)
        v_scr[a:a + 1, :] = m
        hit = rows[0] >= m
        for k in range(min(n_tiles, PEER_RANKS - 1 - a)):
            below = rows[k + 1] if k + 1 < n_tiles else NEG_INF
            rows[k] = jnp.where(hit, below, rows[k])


def _route_kernel(h2_ref, wq_ref, k1_ref, k2_ref, c1_ref, w1_ref, e2_ref,
                  s_scr, v1_scr, v2_scr, *, tm):
    q = jnp.dot(h2_ref[...], wq_ref[...], preferred_element_type=F32).astype(k1_ref.dtype)
    v1_scr[...] = jnp.full(v1_scr.shape, NEG_INF, F32)
    v2_scr[...] = jnp.full(v2_scr.shape, NEG_INF, F32)
    for h in range(PEER_HEADS):
        for half, (k_ref, v_scr) in enumerate(((k1_ref, v1_scr), (k2_ref, v2_scr))):
            col = (2 * h + half) * PEER_HALF
            qh = q[:, col:col + PEER_HALF]
            s = lax.dot_general(k_ref[...], qh, (((1,), (1,)), ((), ())), preferred_element_type=F32)
            s_scr[half] = s
            _top_values(s, v_scr)
        tiles = _candidate_tiles(v1_scr, v2_scr, tm)
        work = list(tiles)
        ranked = []
        for a in range(PEER_RANKS):
            m = work[0]
            for t in work[1:]:
                m = jnp.maximum(m, t)
            m = jnp.max(m, axis=0, keepdims=True)
            ranked.append(m)
            work = [jnp.where(t >= m, NEG_INF, t) for t in work]
        top = ranked[0]
        thr = 0.5 * (ranked[PEER_TOPK - 1] + ranked[PEER_TOPK])
        zsum = jnp.zeros((SUBLANES, tm), F32)
        for t in tiles:
            zsum = zsum + jnp.where(t >= thr, jnp.exp(t - top), 0.0)
        inv_z = 1.0 / jnp.sum(zsum, axis=0, keepdims=True)
        s1 = s_scr[0]
        s2 = s_scr[1]
        top2 = v2_scr[0:1, :]
        c1_ref[h] = jnp.exp(thr - s1 - top2)
        w1_ref[h] = jnp.exp(s1 - v1_scr[0:1, :]) * inv_z
        e2_ref[h] = jnp.exp(s2 - top2)


def _route(h2, wq, k1, k2):
    t, d = h2.shape
    tm = _pick(t, 256)
    kern = functools.partial(_route_kernel, tm=tm)
    const = lambda a: pl.BlockSpec(a.shape, lambda i: (0,) * a.ndim)
    out_spec = pl.BlockSpec((PEER_HEADS, PEER_N_KEYS, tm), lambda i: (0, 0, i))
    out_sds = jax.ShapeDtypeStruct((PEER_HEADS, PEER_N_KEYS, t), F32)
    return pl.pallas_call(
        kern,
        grid=(t // tm,),
        in_specs=[pl.BlockSpec((tm, d), lambda i: (i, 0)), const(wq), const(k1), const(k2)],
        out_specs=[out_spec] * 3,
        out_shape=[out_sds] * 3,
        scratch_shapes=[pltpu.VMEM((2, PEER_N_KEYS, tm), F32),
                        pltpu.VMEM((PEER_RANK_ROWS, tm), F32),
                        pltpu.VMEM((PEER_RANK_ROWS, tm), F32)],
        compiler_params=_params(("parallel",)),
        name="route",
    )(h2, wq, k1, k2)


EXPERT_TILE = 1024
GATE_ROWS = 32


def _expert_gated(i_first, n_i, row0, act_r, c1_ref, w1_ref, e2_ref, tm):
    ps = []
    for ii in range(n_i):
        i = i_first + ii
        rows = slice(row0 + ii * PEER_N_KEYS, row0 + (ii + 1) * PEER_N_KEYS)
        c_rows = [c1_ref[h, pl.ds(i, 1), :] for h in range(PEER_HEADS)]
        w_rows = [w1_ref[h, pl.ds(i, 1), :] for h in range(PEER_HEADS)]
        chunks = []
        for r0 in range(0, PEER_N_KEYS, GATE_ROWS):
            gate = None
            for h in range(PEER_HEADS):
                e2 = e2_ref[h, r0:r0 + GATE_ROWS, :]
                term = jnp.where(e2 >= c_rows[h], e2 * w_rows[h], 0.0)
                gate = term if gate is None else gate + term
            act = act_r[rows.start + r0:rows.start + r0 + GATE_ROWS, :]
            chunks.append((gate * _gelu(act)).astype(BF16))
        ps.append(jnp.concatenate(chunks, axis=0))
    return ps[0] if len(ps) == 1 else jnp.concatenate(ps, axis=0)


def _experts_kernel(u_ref, h2t_ref, vt_ref, c1_ref, w1_ref, e2_ref, x1_ref, mod_ref, o_ref,
                    acc_ref, act0, act1, *, te, tm, ne):
    e = pl.program_id(1)
    n_i = te // PEER_N_KEYS
    slots = (act0, act1)

    def step(act_w, act_r, i0):
        if act_w is not None:
            act_w[...] = jnp.dot(u_ref[...], h2t_ref[...], preferred_element_type=F32)
        if act_r is not None:
            p = _expert_gated(i0, n_i, 0, act_r, c1_ref, w1_ref, e2_ref, tm)
            acc_ref[...] += jnp.dot(vt_ref[0], p, preferred_element_type=F32)

    @pl.when(e == 0)
    def _():
        acc_ref[...] = jnp.zeros_like(acc_ref)
        step(slots[0], None, None)

    for parity in range(2):
        @pl.when(jnp.logical_and(jnp.logical_and(e > 0, e < ne), lax.rem(e, 2) == parity))
        def _(parity=parity):
            step(slots[parity], slots[1 - parity], (e - 1) * n_i)

    @pl.when(e == ne)
    def _():
        step(None, slots[(ne - 1) % 2], (ne - 1) * n_i)
        o_ref[...] = x1_ref[...] + mod_ref[0, 5:6, :] * acc_ref[...].T


def _experts(u_tab, h2t, v_tab, route_out, x1, mod3, seq):
    n_e, d = u_tab.shape
    t = h2t.shape[1]
    tm = _pick(seq, 512)
    te = _pick(n_e, EXPERT_TILE)
    ne = n_e // te
    spb = seq // tm
    vt_tiles = v_tab.reshape(ne, te, d).transpose(0, 2, 1).astype(BF16)
    kern = functools.partial(_experts_kernel, te=te, tm=tm, ne=ne)
    rspec = pl.BlockSpec((PEER_HEADS, PEER_N_KEYS, tm), lambda i, e: (0, 0, i))
    return pl.pallas_call(
        kern,
        grid=(t // tm, ne + 1),
        in_specs=[
            pl.BlockSpec((te, d), lambda i, e: (jnp.minimum(e, ne - 1), 0)),
            pl.BlockSpec((d, tm), lambda i, e: (0, i)),
            pl.BlockSpec((1, d, te), lambda i, e: (jnp.maximum(e - 1, 0), 0, 0)),
            rspec, rspec, rspec,
            pl.BlockSpec((tm, d), lambda i, e: (i, 0), pipeline_mode=pl.Buffered(1)),
            pl.BlockSpec((1, N_ADA, d), lambda i, e: (i // spb, 0, 0)),
        ],
        out_specs=pl.BlockSpec((tm, d), lambda i, e: (i, 0), pipeline_mode=pl.Buffered(1)),
        out_shape=jax.ShapeDtypeStruct((t, d), F32),
        scratch_shapes=[pltpu.VMEM((d, tm), F32), pltpu.VMEM((te, tm), F32), pltpu.VMEM((te, tm), F32)],
        compiler_params=_params(("parallel", "arbitrary")),
        name="experts",
    )(u_tab, h2t, vt_tiles, *route_out, x1, mod3)


def kernel(x, c, w_ada, b_ada, norm1_g, w_in, lam_re, lam_im, log_dt, ssm_b_re, ssm_b_im, ssm_c_re, ssm_c_im, ssm_d, w_glu, q_norm_g, k_norm_g, w_att_up, w_out, norm2_g, peer_wq, peer_k1, peer_k2, peer_u, peer_v):
    bsz, seq, d = x.shape
    depth = w_ada.shape[0]
    ssm_width = ssm_d.shape[1] * ssm_d.shape[2]
    sb_width = SB_HEADS * SB_HEAD_DIM
    assert ssm_width == sb_width and d % sb_width == 0
    assert ssm_d.shape[1] % S5_CHUNKS == 0 and ssm_d.shape[2] == SSM_GROUP and lam_re.shape[2] == SSM_STATE
    q_col = sb_width // SB_HEAD_DIM
    k_col = 2 * q_col
    v_col = 3 * q_col
    gate_base = 4 * sb_width
    assert gate_base % d == 0
    gs_col = gate_base // d
    ga_col = gs_col + 1

    x2 = x.reshape(bsz * seq, d)
    c_pad = jnp.zeros((SUBLANES, d), F32).at[:bsz].set(c)
    for l in range(depth):
        mod = _ada(c_pad, w_ada[l], b_ada[l].reshape(1, -1))
        mod3 = mod.reshape(SUBLANES, N_ADA, d)
        proj = _inproj(x2, mod3, norm1_g[l].reshape(1, d), w_in[l].astype(BF16),
                       q_norm_g[l].reshape(1, -1), k_norm_g[l].reshape(1, -1), seq, sb_width)
        s5_ops = _s5_operands(lam_re[l], lam_im[l], log_dt[l], ssm_b_re[l], ssm_b_im[l],
                              ssm_c_re[l], ssm_c_im[l], ssm_d[l])
        ygelu = _s5(proj, s5_ops, bsz, seq, ssm_width)
        att = _attn(proj, bsz, seq, q_col, k_col, v_col)
        x1, h2, h2t = _mix(ygelu, att, proj, x2, mod3, w_glu[l].astype(BF16), w_att_up[l].astype(BF16),
                           w_out[l].astype(BF16), norm2_g[l].reshape(1, d), seq, gs_col, ga_col)
        route_out = _route(h2, peer_wq[l].astype(BF16), peer_k1[l].astype(BF16), peer_k2[l].astype(BF16))
        x2 = _experts(peer_u[l].astype(BF16), h2t, peer_v[l], route_out, x1, mod3, seq)
    return x2.reshape(bsz, seq, d)
```

```python
import functools
import math

import jax
import jax.numpy as jnp
from jax import lax
from jax.experimental import pallas as pl
from jax.experimental.pallas import tpu as pltpu

F32 = jnp.float32
BF16 = jnp.bfloat16

LANES = 128
SUBLANES = 8
VMEM_LIMIT = 56 * 1024 * 1024

SSM_GROUP = 16
SSM_STATE = 64
SB_HEADS = 8
SB_HEAD_DIM = 128
PEER_HEADS = 8
PEER_N_KEYS = 128
PEER_TOPK = 16
PEER_HALF = 128
N_ADA = 6
EPS = 1e-6
NEG_INF = float("-inf")
LOG2E = math.log2(math.e)


def _params(sem):
    return pltpu.CompilerParams(dimension_semantics=sem, vmem_limit_bytes=VMEM_LIMIT)


GELU_C = -2.0 * math.sqrt(2.0 / math.pi) * LOG2E


def _gelu(x):
    return x / (1.0 + jnp.exp2(x * (GELU_C + (GELU_C * 0.044715) * (x * x))))


def _pick(n, pref):
    t = min(n, pref)
    while n % t:
        t //= 2
    return t


def _ada_kernel(c_ref, w_ref, b_ref, o_ref):
    c = c_ref[...]
    c = c * jax.nn.sigmoid(c)
    o_ref[...] = jnp.dot(c, w_ref[...], preferred_element_type=F32) + b_ref[...]


def _ada(c_pad, w, b):
    rows, d = c_pad.shape
    n = w.shape[1]
    tn = _pick(n, 1024)
    return pl.pallas_call(
        _ada_kernel,
        grid=(n // tn,),
        in_specs=[
            pl.BlockSpec((rows, d), lambda j: (0, 0)),
            pl.BlockSpec((d, tn), lambda j: (0, j)),
            pl.BlockSpec((1, tn), lambda j: (0, j)),
        ],
        out_specs=pl.BlockSpec((rows, tn), lambda j: (0, j)),
        out_shape=jax.ShapeDtypeStruct((rows, n), F32),
        compiler_params=_params(("arbitrary",)),
        name="ada",
    )(c_pad, w, b)


def _head_rms(acc, g_ref, o_ref, scale):
    for h in range(acc.shape[1] // SB_HEAD_DIM):
        blk = acc[:, h * SB_HEAD_DIM:(h + 1) * SB_HEAD_DIM]
        ms = jnp.mean(blk * blk, axis=-1, keepdims=True)
        y = blk * lax.rsqrt(ms + EPS) * g_ref[...]
        if scale != 1.0:
            y = y * scale
        o_ref[:, h * SB_HEAD_DIM:(h + 1) * SB_HEAD_DIM] = y.astype(o_ref.dtype)


def _inproj_kernel(x_ref, mod_ref, g_ref, w_ref, qg_ref, kg_ref, o_ref, h_scr, *, q_blk, k_blk):
    j = pl.program_id(1)

    @pl.when(j == 0)
    def _():
        x = x_ref[...]
        ms = jnp.mean(x * x, axis=-1, keepdims=True)
        y = x * lax.rsqrt(ms + EPS) * g_ref[...]
        h = y * (1.0 + mod_ref[0, 1:2, :]) + mod_ref[0, 0:1, :]
        h_scr[...] = h.astype(BF16)

    acc = jnp.dot(h_scr[...], w_ref[...], preferred_element_type=F32)

    @pl.when(j == q_blk)
    def _():
        _head_rms(acc, qg_ref, o_ref, SB_HEAD_DIM ** -0.5 * LOG2E)

    @pl.when(j == k_blk)
    def _():
        _head_rms(acc, kg_ref, o_ref, 1.0)

    @pl.when(jnp.logical_and(j != q_blk, j != k_blk))
    def _():
        o_ref[...] = acc.astype(o_ref.dtype)


def _inproj(x2, mod3, g, w, qg, kg, seq, sb_width):
    t, d = x2.shape
    n = w.shape[1]
    tm = _pick(seq, 1024)
    tn = sb_width
    steps_per_batch = seq // tm
    kern = functools.partial(_inproj_kernel, q_blk=1, k_blk=2)
    return pl.pallas_call(
        kern,
        grid=(t // tm, n // tn),
        in_specs=[
            pl.BlockSpec((tm, d), lambda i, j: (i, 0)),
            pl.BlockSpec((1, N_ADA, d), lambda i, j: (i // steps_per_batch, 0, 0)),
            pl.BlockSpec((1, d), lambda i, j: (0, 0)),
            pl.BlockSpec((d, tn), lambda i, j: (0, j)),
            pl.BlockSpec((1, SB_HEAD_DIM), lambda i, j: (0, 0)),
            pl.BlockSpec((1, SB_HEAD_DIM), lambda i, j: (0, 0)),
        ],
        out_specs=pl.BlockSpec((tm, tn), lambda i, j: (i, j)),
        out_shape=jax.ShapeDtypeStruct((t, n), BF16),
        scratch_shapes=[pltpu.VMEM((tm, d), BF16)],
        compiler_params=_params(("parallel", "arbitrary")),
        name="inproj",
    )(x2, mod3, g, w, qg, kg)


S5_CHUNKS = 8
S5_CHUNK_IN = LANES
S5_CHUNK_STATE = 8 * SSM_STATE
S5_SLABS = S5_CHUNK_STATE // LANES
S5_PITCH_PAD = SUBLANES


def _s5_kernel(u_ref, br_ref, bi_ref, cr_ref, ci_ref, lamr_ref, lami_ref, d_ref, o_ref,
               bur, bui, xrs, xis, sr, si, *, tm, pitch):
    @pl.when(pl.program_id(1) == 0)
    def _():
        sr[...] = jnp.zeros_like(sr)
        si[...] = jnp.zeros_like(si)

    for c in range(S5_CHUNKS):
        uc = u_ref[:, c * S5_CHUNK_IN:(c + 1) * S5_CHUNK_IN]
        r = jnp.dot(uc, br_ref[c], preferred_element_type=F32)
        im = jnp.dot(uc, bi_ref[c], preferred_element_type=F32)
        for l in range(S5_SLABS):
            bur[l, c * pitch:c * pitch + tm, :] = r[:, l * LANES:(l + 1) * LANES]
            bui[l, c * pitch:c * pitch + tm, :] = im[:, l * LANES:(l + 1) * LANES]

    lamr = [lamr_ref[:, l * LANES:(l + 1) * LANES] for l in range(S5_SLABS)]
    lami = [lami_ref[:, l * LANES:(l + 1) * LANES] for l in range(S5_SLABS)]

    def body(t, carry):
        xr, xi = carry
        nxr, nxi = [], []
        for l in range(S5_SLABS):
            rows = pl.ds(t, S5_CHUNKS, stride=pitch)
            b_r = bur[l, rows, :]
            b_i = bui[l, rows, :]
            n_r = lamr[l] * xr[l] - lami[l] * xi[l] + b_r
            n_i = lamr[l] * xi[l] + lami[l] * xr[l] + b_i
            xrs[l, rows, :] = n_r
            xis[l, rows, :] = n_i
            nxr.append(n_r)
            nxi.append(n_i)
        return tuple(nxr), tuple(nxi)

    init = (tuple(sr[:, l * LANES:(l + 1) * LANES] for l in range(S5_SLABS)),
            tuple(si[:, l * LANES:(l + 1) * LANES] for l in range(S5_SLABS)))
    xr, xi = lax.fori_loop(0, tm, body, init, unroll=8)
    for l in range(S5_SLABS):
        sr[:, l * LANES:(l + 1) * LANES] = xr[l]
        si[:, l * LANES:(l + 1) * LANES] = xi[l]

    for c in range(S5_CHUNKS):
        state = jnp.concatenate(
            [ref[l, c * pitch:c * pitch + tm, :].astype(BF16) for ref in (xrs, xis) for l in range(S5_SLABS)],
            axis=1)
        weight = jnp.concatenate([cr_ref[c], ci_ref[c]], axis=0)
        acc = jnp.dot(state, weight, preferred_element_type=F32)
        cols = slice(c * S5_CHUNK_IN, (c + 1) * S5_CHUNK_IN)
        y = acc + d_ref[:, cols] * u_ref[:, cols].astype(F32)
        o_ref[:, cols] = _gelu(y).astype(o_ref.dtype)


def _s5_operands(lam_re, lam_im, log_dt, b_re, b_im, c_re, c_im, d_skip):
    g, p = lam_re.shape
    i_dim = b_re.shape[-1]
    dt = jnp.exp(log_dt)[:, None]
    mag = jnp.exp(lam_re * dt)
    lbr = mag * jnp.cos(lam_im * dt)
    lbi = mag * jnp.sin(lam_im * dt)
    nr, ni = lbr - 1.0, lbi
    den = lam_re * lam_re + lam_im * lam_im
    cfr = (nr * lam_re + ni * lam_im) / den
    cfi = (ni * lam_re - nr * lam_im) / den
    bbr = cfr[..., None] * b_re - cfi[..., None] * b_im
    bbi = cfr[..., None] * b_im + cfi[..., None] * b_re
    gpc = g // S5_CHUNKS
    eye = jnp.eye(gpc, dtype=F32)

    def pack_b(m):
        m = m.reshape(S5_CHUNKS, gpc, p, i_dim)
        return jnp.einsum("cgpi,gh->cgihp", m, eye).reshape(S5_CHUNKS, gpc * i_dim, gpc * p)

    def pack_c(m):
        m = m.reshape(S5_CHUNKS, gpc, i_dim, p)
        return jnp.einsum("cgip,gh->cgphi", m, eye).reshape(S5_CHUNKS, gpc * p, gpc * i_dim)

    return (pack_b(bbr).astype(BF16), pack_b(bbi).astype(BF16),
            pack_c(c_re).astype(BF16), pack_c(-c_im).astype(BF16),
            lbr.reshape(S5_CHUNKS, gpc * p), lbi.reshape(S5_CHUNKS, gpc * p),
            d_skip.reshape(1, g * i_dim))


def _s5(proj, ops, bsz, seq, width):
    br, bi, cr, ci, lamr, lami, dsk = ops
    tm = _pick(seq, 256)
    pitch = tm + S5_PITCH_PAD
    nt = seq // tm
    kern = functools.partial(_s5_kernel, tm=tm, pitch=pitch)
    full = lambda a: pl.BlockSpec(a.shape, lambda b, i: (0,) * a.ndim)
    scr = pltpu.VMEM((S5_SLABS, S5_CHUNKS * pitch, LANES), F32)
    return pl.pallas_call(
        kern,
        grid=(bsz, nt),
        in_specs=[pl.BlockSpec((tm, width), lambda b, i: (b * nt + i, 0)),
                  full(br), full(bi), full(cr), full(ci), full(lamr), full(lami), full(dsk)],
        out_specs=pl.BlockSpec((tm, width), lambda b, i: (b * nt + i, 0)),
        out_shape=jax.ShapeDtypeStruct((bsz * seq, width), BF16),
        scratch_shapes=[scr, scr, scr, scr,
                        pltpu.VMEM((S5_CHUNKS, S5_CHUNK_STATE), F32),
                        pltpu.VMEM((S5_CHUNKS, S5_CHUNK_STATE), F32)],
        compiler_params=_params(("parallel", "arbitrary")),
        name="s5",
    )(proj, br, bi, cr, ci, lamr, lami, dsk)


ATTN_GROUP = 2
ATTN_HEADS = 4
ATTN_DEAD_CARRY = 160.0


def _neg_abs(z):
    bits = lax.bitcast_convert_type(z, jnp.uint32) | jnp.uint32(0x80000000)
    return lax.bitcast_convert_type(bits, F32)


def _attn_kernel(q_ref, k_ref, v_ref, tri_ref, o_ref, acc_ref, carry_ref, *, tb):
    qi = pl.program_id(2)
    acc_ref[...] = jnp.zeros_like(acc_ref)
    carry_ref[...] = jnp.zeros_like(carry_ref)

    def group(kb_lo, n, diagonal):
        ks = pl.multiple_of(kb_lo * tb, tb)
        for hh in range(ATTN_HEADS):
            cols = slice(hh * SB_HEAD_DIM, (hh + 1) * SB_HEAD_DIM)
            q = q_ref[:, cols]
            k = k_ref[pl.ds(ks, n * tb), cols]
            v = v_ref[pl.ds(ks, n * tb), cols]
            carry = carry_ref[hh]
            ws = [None] * n
            for g in reversed(range(n)):
                z = lax.dot_general(q, k[g * tb:(g + 1) * tb, :], (((1,), (1,)), ((), ())),
                                    preferred_element_type=F32)
                sp = jnp.maximum(z, 0.0) + jnp.log(1.0 + jnp.exp2(_neg_abs(z))) * LOG2E
                masked = diagonal and g == n - 1
                if masked:
                    row = lax.broadcasted_iota(jnp.int32, (tb, tb), 0)
                    col = lax.broadcasted_iota(jnp.int32, (tb, tb), 1)
                    causal = col < row
                    sp_sum = jnp.where(causal, sp, 0.0)
                else:
                    sp_sum = sp
                cum = jnp.dot(sp_sum.astype(BF16), tri_ref[...], preferred_element_type=F32)
                w = jnp.exp2(z - sp - cum - carry)
                if masked:
                    w = jnp.where(causal, w, 0.0)
                ws[g] = w.astype(BF16)
                carry = carry + jnp.sum(sp_sum, axis=1, keepdims=True)
            w_all = ws[0] if n == 1 else jnp.concatenate(ws, axis=1)
            acc_ref[:, cols] += jnp.dot(w_all, v, preferred_element_type=F32)
            carry_ref[hh] = carry

    def live():
        return (jnp.min(carry_ref[...]) < ATTN_DEAD_CARRY).astype(jnp.int32)

    group(qi, 1, True)

    @pl.when(jnp.logical_and(qi > 0, live() > 0))
    def _():
        group(qi - 1, 1, False)

    @pl.when(jnp.logical_and(qi > 1, live() > 0))
    def _():
        rest = qi - 1
        done = jnp.int32(0)
        size = 1
        while size < ATTN_GROUP:
            bit = (rest & size) != 0

            @pl.when(jnp.logical_and(bit, live() > 0))
            def _(done=done, size=size):
                group(rest - done - size, size, False)

            done = done + jnp.where(bit, size, 0)
            size *= 2

        n_full = rest // ATTN_GROUP

        def more(state):
            it, alive = state
            return jnp.logical_and(it < n_full, alive > 0)

        def multi(state):
            it, _ = state
            group(rest - done - (it + 1) * ATTN_GROUP, ATTN_GROUP, False)
            return it + 1, live()

        lax.while_loop(more, multi, (jnp.int32(0), live()))

    o_ref[...] = acc_ref[...].astype(o_ref.dtype)


def _attn(proj, bsz, seq, q_col, k_col, v_col):
    tb = _pick(seq, 256)
    nq = seq // tb
    r = lax.broadcasted_iota(jnp.int32, (tb, tb), 0)
    c = lax.broadcasted_iota(jnp.int32, (tb, tb), 1)
    tri = jnp.where(r > c, 1.0, 0.0).astype(BF16)
    kern = functools.partial(_attn_kernel, tb=tb)
    hw = ATTN_HEADS * SB_HEAD_DIM
    assert q_col % ATTN_HEADS == 0 and SB_HEADS % ATTN_HEADS == 0
    qc, kc, vc = q_col // ATTN_HEADS, k_col // ATTN_HEADS, v_col // ATTN_HEADS
    return pl.pallas_call(
        kern,
        grid=(bsz, SB_HEADS // ATTN_HEADS, nq),
        in_specs=[
            pl.BlockSpec((tb, hw), lambda b, h, i: (b * nq + i, qc + h)),
            pl.BlockSpec((seq, hw), lambda b, h, i: (b, kc + h)),
            pl.BlockSpec((seq, hw), lambda b, h, i: (b, vc + h)),
            pl.BlockSpec((tb, tb), lambda b, h, i: (0, 0)),
        ],
        out_specs=pl.BlockSpec((tb, hw), lambda b, h, i: (b * nq + i, h)),
        out_shape=jax.ShapeDtypeStruct((bsz * seq, SB_HEADS * SB_HEAD_DIM), BF16),
        scratch_shapes=[pltpu.VMEM((tb, hw), F32), pltpu.VMEM((ATTN_HEADS, tb, 1), F32)],
        compiler_params=_params(("parallel", "parallel", "arbitrary")),
        name="attn",
    )(proj, proj, proj, tri)


def _mix_kernel(y_ref, att_ref, gs_ref, ga_ref, x_ref, mod_ref, wglu_ref, wup_ref, wout_ref, g2_ref,
                x1_ref, h2_ref, h2t_ref, *, d):
    yg = jnp.dot(y_ref[...], wglu_ref[...], preferred_element_type=F32)
    ssm = yg[:, :d] * jax.nn.sigmoid(yg[:, d:])
    up = jnp.dot(att_ref[...], wup_ref[...], preferred_element_type=F32)
    merged = (jax.nn.sigmoid(gs_ref[...].astype(F32)) * ssm
              + jax.nn.sigmoid(ga_ref[...].astype(F32)) * up)
    o = jnp.dot(merged.astype(BF16), wout_ref[...], preferred_element_type=F32)
    x1 = x_ref[...] + mod_ref[0, 2:3, :] * o
    x1_ref[...] = x1
    ms = jnp.mean(x1 * x1, axis=-1, keepdims=True)
    h2 = x1 * lax.rsqrt(ms + EPS) * g2_ref[...]
    h2 = h2 * (1.0 + mod_ref[0, 4:5, :]) + mod_ref[0, 3:4, :]
    h2_ref[...] = h2.astype(BF16)
    h2t_ref[...] = h2.T.astype(BF16)


def _mix(ygelu, att, proj, x2, mod3, wglu, wup, wout, g2, seq, gs_col, ga_col):
    t, d = x2.shape
    w = ygelu.shape[1]
    tm = _pick(seq, 256)
    spb = seq // tm
    kern = functools.partial(_mix_kernel, d=d)
    const = lambda a: pl.BlockSpec(a.shape, lambda i: (0,) * a.ndim)
    return pl.pallas_call(
        kern,
        grid=(t // tm,),
        in_specs=[
            pl.BlockSpec((tm, w), lambda i: (i, 0)),
            pl.BlockSpec((tm, w), lambda i: (i, 0)),
            pl.BlockSpec((tm, d), lambda i: (i, gs_col)),
            pl.BlockSpec((tm, d), lambda i: (i, ga_col)),
            pl.BlockSpec((tm, d), lambda i: (i, 0)),
            pl.BlockSpec((1, N_ADA, d), lambda i: (i // spb, 0, 0)),
            const(wglu), const(wup), const(wout), const(g2),
        ],
        out_specs=[
            pl.BlockSpec((tm, d), lambda i: (i, 0)),
            pl.BlockSpec((tm, d), lambda i: (i, 0)),
            pl.BlockSpec((d, tm), lambda i: (0, i)),
        ],
        out_shape=[jax.ShapeDtypeStruct((t, d), F32),
                   jax.ShapeDtypeStruct((t, d), BF16),
                   jax.ShapeDtypeStruct((d, t), BF16)],
        compiler_params=_params(("parallel",)),
        name="mix",
    )(ygelu, att, proj, proj, x2, mod3, wglu, wup, wout, g2)


PEER_RANKS = PEER_TOPK + 1
PEER_RANK_ROWS = 24


def _candidate_tiles(v1_scr, v2_scr, tm):
    tiles = []
    row8 = lax.broadcasted_iota(jnp.int32, (SUBLANES, tm), 0)
    for b in range(PEER_RANKS):
        n_a = PEER_RANKS // (b + 1)
        v2b = v2_scr[b:b + 1, :]
        for base in range(0, n_a, SUBLANES):
            t = v1_scr[base:base + SUBLANES, :] + v2b
            if n_a - base < SUBLANES:
                t = jnp.where(row8 < (n_a - base), t, NEG_INF)
            tiles.append(t)
    return tiles


def _sort_network(n):
    pairs = []
    p = 1
    while p < n:
        k = p
        while k >= 1:
            for j in range(k % p, n - k, 2 * k):
                for i in range(min(k, n - j - k)):
                    if (i + j) // (2 * p) == (i + j + k) // (2 * p):
                        pairs.append((i + j, i + j + k))
            k //= 2
        p *= 2
    return pairs


def _top_values(s, v_scr):
    n_tiles = s.shape[0] // SUBLANES
    rows = [s[k * SUBLANES:(k + 1) * SUBLANES, :] for k in range(n_tiles)]
    for i, j in _sort_network(n_tiles):
        rows[i], rows[j] = jnp.maximum(rows[i], rows[j]), jnp.minimum(rows[i], rows[j])
    for a in range(PEER_RANKS):
        m = jnp.max(rows[0], axis=0, keepdims=True)
        v_scr[a:a + 1, :] = m
        hit = rows[0] >= m
        for k in range(min(n_tiles, PEER_RANKS - 1 - a)):
            below = rows[k + 1] if k + 1 < n_tiles else NEG_INF
            rows[k] = jnp.where(hit, below, rows[k])


def _route_kernel(h2_ref, wq_ref, k1_ref, k2_ref, c1_ref, w1_ref, e2_ref,
                  s_scr, v1_scr, v2_scr, *, tm):
    q = jnp.dot(h2_ref[...], wq_ref[...], preferred_element_type=F32).astype(k1_ref.dtype)
    v1_scr[...] = jnp.full(v1_scr.shape, NEG_INF, F32)
    v2_scr[...] = jnp.full(v2_scr.shape, NEG_INF, F32)
    for h in range(PEER_HEADS):
        for half, (k_ref, v_scr) in enumerate(((k1_ref, v1_scr), (k2_ref, v2_scr))):
            col = (2 * h + half) * PEER_HALF
            qh = q[:, col:col + PEER_HALF]
            s = lax.dot_general(k_ref[...], qh, (((1,), (1,)), ((), ())), preferred_element_type=F32)
            s_scr[half] = s
            _top_values(s, v_scr)
        tiles = _candidate_tiles(v1_scr, v2_scr, tm)
        work = list(tiles)
        ranked = []
        for a in range(PEER_RANKS):
            m = work[0]
            for t in work[1:]:
                m = jnp.maximum(m, t)
            m = jnp.max(m, axis=0, keepdims=True)
            ranked.append(m)
            work = [jnp.where(t >= m, NEG_INF, t) for t in work]
        top = ranked[0]
        thr = 0.5 * (ranked[PEER_TOPK - 1] + ranked[PEER_TOPK])
        zsum = jnp.zeros((SUBLANES, tm), F32)
        for t in tiles:
            zsum = zsum + jnp.where(t >= thr, jnp.exp(t - top), 0.0)
        inv_z = 1.0 / jnp.sum(zsum, axis=0, keepdims=True)
        s1 = s_scr[0]
        s2 = s_scr[1]
        top2 = v2_scr[0:1, :]
        c1_ref[h] = jnp.exp(thr - s1 - top2)
        w1_ref[h] = jnp.exp(s1 - v1_scr[0:1, :]) * inv_z
        e2_ref[h] = jnp.exp(s2 - top2)


def _route(h2, wq, k1, k2):
    t, d = h2.shape
    tm = _pick(t, 256)
    kern = functools.partial(_route_kernel, tm=tm)
    const = lambda a: pl.BlockSpec(a.shape, lambda i: (0,) * a.ndim)
    out_spec = pl.BlockSpec((PEER_HEADS, PEER_N_KEYS, tm), lambda i: (0, 0, i))
    out_sds = jax.ShapeDtypeStruct((PEER_HEADS, PEER_N_KEYS, t), F32)
    return pl.pallas_call(
        kern,
        grid=(t // tm,),
        in_specs=[pl.BlockSpec((tm, d), lambda i: (i, 0)), const(wq), const(k1), const(k2)],
        out_specs=[out_spec] * 3,
        out_shape=[out_sds] * 3,
        scratch_shapes=[pltpu.VMEM((2, PEER_N_KEYS, tm), F32),
                        pltpu.VMEM((PEER_RANK_ROWS, tm), F32),
                        pltpu.VMEM((PEER_RANK_ROWS, tm), F32)],
        compiler_params=_params(("parallel",)),
        name="route",
    )(h2, wq, k1, k2)


EXPERT_TILE = 1024
GATE_ROWS = 32


def _expert_gated(i_first, n_i, row0, act_r, c1_ref, w1_ref, e2_ref, tm):
    ps = []
    for ii in range(n_i):
        i = i_first + ii
        rows = slice(row0 + ii * PEER_N_KEYS, row0 + (ii + 1) * PEER_N_KEYS)
        c_rows = [c1_ref[h, pl.ds(i, 1), :] for h in range(PEER_HEADS)]
        w_rows = [w1_ref[h, pl.ds(i, 1), :] for h in range(PEER_HEADS)]
        chunks = []
        for r0 in range(0, PEER_N_KEYS, GATE_ROWS):
            gate = None
            for h in range(PEER_HEADS):
                e2 = e2_ref[h, r0:r0 + GATE_ROWS, :]
                term = jnp.where(e2 >= c_rows[h], e2 * w_rows[h], 0.0)
                gate = term if gate is None else gate + term
            act = act_r[rows.start + r0:rows.start + r0 + GATE_ROWS, :]
            chunks.append((gate * _gelu(act)).astype(BF16))
        ps.append(jnp.concatenate(chunks, axis=0))
    return ps[0] if len(ps) == 1 else jnp.concatenate(ps, axis=0)


def _experts_kernel(u_ref, h2t_ref, vt_ref, c1_ref, w1_ref, e2_ref, x1_ref, mod_ref, o_ref,
                    acc_ref, act0, act1, *, te, tm, ne):
    e = pl.program_id(1)
    n_i = te // PEER_N_KEYS
    slots = (act0, act1)

    def step(act_w, act_r, i0):
        if act_w is not None:
            act_w[...] = jnp.dot(u_ref[...], h2t_ref[...], preferred_element_type=F32)
        if act_r is not None:
            p = _expert_gated(i0, n_i, 0, act_r, c1_ref, w1_ref, e2_ref, tm)
            acc_ref[...] += jnp.dot(vt_ref[0], p, preferred_element_type=F32)

    @pl.when(e == 0)
    def _():
        acc_ref[...] = jnp.zeros_like(acc_ref)
        step(slots[0], None, None)

    for parity in range(2):
        @pl.when(jnp.logical_and(jnp.logical_and(e > 0, e < ne), lax.rem(e, 2) == parity))
        def _(parity=parity):
            step(slots[parity], slots[1 - parity], (e - 1) * n_i)

    @pl.when(e == ne)
    def _():
        step(None, slots[(ne - 1) % 2], (ne - 1) * n_i)
        o_ref[...] = x1_ref[...] + mod_ref[0, 5:6, :] * acc_ref[...].T


def _experts(u_tab, h2t, v_tab, route_out, x1, mod3, seq):
    n_e, d = u_tab.shape
    t = h2t.shape[1]
    tm = _pick(seq, 512)
    te = _pick(n_e, EXPERT_TILE)
    ne = n_e // te
    spb = seq // tm
    vt_tiles = v_tab.reshape(ne, te, d).transpose(0, 2, 1).astype(BF16)
    kern = functools.partial(_experts_kernel, te=te, tm=tm, ne=ne)
    rspec = pl.BlockSpec((PEER_HEADS, PEER_N_KEYS, tm), lambda i, e: (0, 0, i))
    return pl.pallas_call(
        kern,
        grid=(t // tm, ne + 1),
        in_specs=[
            pl.BlockSpec((te, d), lambda i, e: (jnp.minimum(e, ne - 1), 0)),
            pl.BlockSpec((d, tm), lambda i, e: (0, i)),
            pl.BlockSpec((1, d, te), lambda i, e: (jnp.maximum(e - 1, 0), 0, 0)),
            rspec, rspec, rspec,
            pl.BlockSpec((tm, d), lambda i, e: (i, 0), pipeline_mode=pl.Buffered(1)),
            pl.BlockSpec((1, N_ADA, d), lambda i, e: (i // spb, 0, 0)),
        ],
        out_specs=pl.BlockSpec((tm, d), lambda i, e: (i, 0), pipeline_mode=pl.Buffered(1)),
        out_shape=jax.ShapeDtypeStruct((t, d), F32),
        scratch_shapes=[pltpu.VMEM((d, tm), F32), pltpu.VMEM((te, tm), F32), pltpu.VMEM((te, tm), F32)],
        compiler_params=_params(("parallel", "arbitrary")),
        name="experts",
    )(u_tab, h2t, vt_tiles, *route_out, x1, mod3)


def kernel(x, c, w_ada, b_ada, norm1_g, w_in, lam_re, lam_im, log_dt, ssm_b_re, ssm_b_im, ssm_c_re, ssm_c_im, ssm_d, w_glu, q_norm_g, k_norm_g, w_att_up, w_out, norm2_g, peer_wq, peer_k1, peer_k2, peer_u, peer_v):
    bsz, seq, d = x.shape
    depth = w_ada.shape[0]
    ssm_width = ssm_d.shape[1] * ssm_d.shape[2]
    sb_width = SB_HEADS * SB_HEAD_DIM
    assert ssm_width == sb_width and d % sb_width == 0
    assert ssm_d.shape[1] % S5_CHUNKS == 0 and ssm_d.shape[2] == SSM_GROUP and lam_re.shape[2] == SSM_STATE
    q_col = sb_width // SB_HEAD_DIM
    k_col = 2 * q_col
    v_col = 3 * q_col
    gate_base = 4 * sb_width
    assert gate_base % d == 0
    gs_col = gate_base // d
    ga_col = gs_col + 1

    x2 = x.reshape(bsz * seq, d)
    c_pad = jnp.zeros((SUBLANES, d), F32).at[:bsz].set(c)
    for l in range(depth):
        mod = _ada(c_pad, w_ada[l], b_ada[l].reshape(1, -1))
        mod3 = mod.reshape(SUBLANES, N_ADA, d)
        proj = _inproj(x2, mod3, norm1_g[l].reshape(1, d), w_in[l].astype(BF16),
                       q_norm_g[l].reshape(1, -1), k_norm_g[l].reshape(1, -1), seq, sb_width)
        s5_ops = _s5_operands(lam_re[l], lam_im[l], log_dt[l], ssm_b_re[l], ssm_b_im[l],
                              ssm_c_re[l], ssm_c_im[l], ssm_d[l])
        ygelu = _s5(proj, s5_ops, bsz, seq, ssm_width)
        att = _attn(proj, bsz, seq, q_col, k_col, v_col)
        x1, h2, h2t = _mix(ygelu, att, proj, x2, mod3, w_glu[l].astype(BF16), w_att_up[l].astype(BF16),
                           w_out[l].astype(BF16), norm2_g[l].reshape(1, d), seq, gs_col, ga_col)
        route_out = _route(h2, peer_wq[l].astype(BF16), peer_k1[l].astype(BF16), peer_k2[l].astype(BF16))
        x2 = _experts(peer_u[l].astype(BF16), h2t, peer_v[l], route_out, x1, mod3, seq)
    return x2.reshape(bsz, seq, d)
```
